```python
import math
import jax, jax.numpy as jnp
from jax import lax
import numpy as np

D_MODEL = 1024
BATCH = 8
SEQ = 2048
DEPTH = 4

MEM_LEN = 256
RNN_WIDTH = D_MODEL
RNN_BLOCKS = 4
RNN_BLOCK = RNN_WIDTH // RNN_BLOCKS
CONV_WIDTH = 4
LRU_C = 8.0
HEAD_DIM = 64
N_Q_HEADS = D_MODEL // HEAD_DIM
N_KV_HEADS = 2
GROUP = N_Q_HEADS // N_KV_HEADS
ATTN_WIDTH = N_Q_HEADS * HEAD_DIM
KV_WIDTH = N_KV_HEADS * HEAD_DIM
WINDOW = 128
BLOCK = 128
ROPE_THETA = 500000.0
ROT_DIM = HEAD_DIM // 4
IN_COLS = 2 * RNN_WIDTH + ATTN_WIDTH + 2 * KV_WIDTH + 2 * D_MODEL
CROSS_HEADS = 4
CROSS_HEAD_DIM = D_MODEL // CROSS_HEADS
CROSS_WIDTH = CROSS_HEADS * CROSS_HEAD_DIM
D_FF = -(-8 * D_MODEL // (3 * 256)) * 256
LN_EPS = 1e-5
DEEPNORM_ALPHA = (2 * DEPTH) ** 0.25
DEEPNORM_BETA = (8 * DEPTH) ** -0.25
NEG_INF = -1e30

kernel_name = "hawk_swa_sink_hybrid_deepnorm_trunk"


def layer_norm(x, g, b):
    xf = x.astype(jnp.float32)
    mu = jnp.mean(xf, axis=-1, keepdims=True)
    var = jnp.mean(jnp.square(xf - mu), axis=-1, keepdims=True)
    y = (xf - mu) * lax.rsqrt(var + LN_EPS)
    return (y * g.astype(jnp.float32) + b.astype(jnp.float32)).astype(x.dtype)


def rope_tables(seq_len):
    pos = jnp.arange(seq_len, dtype=jnp.float32)
    inv_freq = ROPE_THETA ** (-jnp.arange(0, ROT_DIM, 2, dtype=jnp.float32) / ROT_DIM)
    ang = pos[:, None] * inv_freq[None, :]
    return jnp.cos(ang), jnp.sin(ang)


def apply_partial_rope(t, cos, sin):
    half = ROT_DIM // 2
    c = cos[None, :, None, :].astype(t.dtype)
    s = sin[None, :, None, :].astype(t.dtype)
    t1, t2, rest = t[..., :half], t[..., half:ROT_DIM], t[..., ROT_DIM:]
    return jnp.concatenate([t1 * c - t2 * s, t2 * c + t1 * s, rest], axis=-1)


def rglru_branch(xr, gr, conv_w, conv_b, w_rg, b_rg, w_ig, b_ig, lru_lambda):
    B, S, _ = xr.shape
    xp = jnp.pad(xr, ((0, 0), (CONV_WIDTH - 1, 0), (0, 0)))
    xc = conv_b
    for k in range(CONV_WIDTH):
        xc = xc + xp[:, k:k + S] * conv_w[k]
    xb = xc.reshape(B, S, RNN_BLOCKS, RNN_BLOCK)
    r = jax.nn.sigmoid(jnp.einsum('bsnc,ncd->bsnd', xb, w_rg).reshape(B, S, RNN_WIDTH) + b_rg)
    i = jax.nn.sigmoid(jnp.einsum('bsnc,ncd->bsnd', xb, w_ig).reshape(B, S, RNN_WIDTH) + b_ig)
    log_a = -LRU_C * r.astype(jnp.float32) * jax.nn.softplus(-lru_lambda.astype(jnp.float32))
    a = jnp.exp(log_a)
    mult = jnp.sqrt(-jnp.expm1(2.0 * log_a))
    b_in = mult * (i * xc).astype(jnp.float32)

    def combine(lhs, rhs):
        a1, b1 = lhs
        a2, b2 = rhs
        return a1 * a2, a2 * b1 + b2

    _, h = lax.associative_scan(combine, (a, b_in), axis=1)
    return h.astype(xr.dtype) * jax.nn.gelu(gr)


def swa_sink_branch(q, k, v, sinks, cos, sin):
    B, S, _ = q.shape
    NB = S // BLOCK
    q = apply_partial_rope(q.reshape(B, S, N_Q_HEADS, HEAD_DIM), cos, sin)
    k = apply_partial_rope(k.reshape(B, S, N_KV_HEADS, HEAD_DIM), cos, sin)
    v = v.reshape(B, S, N_KV_HEADS, HEAD_DIM)
    qb = q.reshape(B, NB, BLOCK, N_KV_HEADS, GROUP, HEAD_DIM)

    def band(t):
        tp = jnp.pad(t, ((0, 0), (BLOCK, 0), (0, 0), (0, 0))).reshape(B, NB + 1, BLOCK, N_KV_HEADS, HEAD_DIM)
        return jnp.concatenate([tp[:, :-1], tp[:, 1:]], axis=2)

    kb, vb = band(k), band(v)
    scores = jnp.einsum('bnqhgd,bnjhd->bnhgqj', qb, kb).astype(jnp.float32) * (HEAD_DIM ** -0.5)
    blk = jnp.arange(NB)[:, None, None]
    qpos = blk * BLOCK + jnp.arange(BLOCK)[None, :, None]
    kpos = (blk - 1) * BLOCK + jnp.arange(2 * BLOCK)[None, None, :]
    valid = (kpos <= qpos) & (kpos > qpos - WINDOW) & (kpos >= 0)
    scores = jnp.where(valid[None, :, None, None], scores, NEG_INF)
    sink = sinks.astype(jnp.float32).reshape(N_KV_HEADS, GROUP)[None, None, :, :, None, None]
    sink = jnp.broadcast_to(sink, scores.shape[:-1] + (1,))
    probs = jax.nn.softmax(jnp.concatenate([scores, sink], axis=-1), axis=-1)[..., :-1]
    out = jnp.einsum('bnhgqj,bnjhd->bnqhgd', probs.astype(vb.dtype), vb)
    return out.reshape(B, S, ATTN_WIDTH)


def hybrid_mixer(u, w_in, conv_w, conv_b, w_rg, b_rg, w_ig, b_ig, lru_lambda,
                 w_br_rnn, w_br_attn, sinks, w_out, cos, sin):
    widths = (RNN_WIDTH, RNN_WIDTH, ATTN_WIDTH, KV_WIDTH, KV_WIDTH, D_MODEL, D_MODEL)
    points = np.cumsum(widths)[:-1].tolist()
    proj = u @ w_in
    xr, gr, q, k, v, g_rnn, g_attn = jnp.split(proj, points, axis=-1)
    y_rnn = rglru_branch(xr, gr, conv_w, conv_b, w_rg, b_rg, w_ig, b_ig, lru_lambda)
    y_attn = swa_sink_branch(q, k, v, sinks, cos, sin)
    merged = jax.nn.sigmoid(g_rnn) * (y_rnn @ w_br_rnn) + jax.nn.sigmoid(g_attn) * (y_attn @ w_br_attn)
    return merged @ w_out


def cross_attention(u, mem, cq_w, ckv_w, co_w):
    B, S, _ = u.shape
    M = mem.shape[1]
    q = (u @ cq_w).reshape(B, S, CROSS_HEADS, CROSS_HEAD_DIM)
    k, v = jnp.split(mem @ ckv_w, 2, axis=-1)
    k = k.reshape(B, M, CROSS_HEADS, CROSS_HEAD_DIM)
    v = v.reshape(B, M, CROSS_HEADS, CROSS_HEAD_DIM)
    s = jnp.einsum('bshd,bmhd->bhsm', q, k).astype(jnp.float32) * (CROSS_HEAD_DIM ** -0.5)
    p = jax.nn.softmax(s, axis=-1)
    o = jnp.einsum('bhsm,bmhd->bshd', p.astype(v.dtype), v).reshape(B, S, CROSS_WIDTH)
    return o @ co_w


def swiglu(u, wi, wo):
    gate, up = jnp.split(u @ wi, 2, axis=-1)
    return (jax.nn.silu(gate) * up) @ wo


def setup_inputs(seed: int = 0) -> dict:
    key = jax.random.key(seed)
    ks = jax.random.split(key, 26)
    L = DEPTH
    f32 = jnp.float32

    def nrm(k, shape, scale):
        return jax.random.normal(k, shape, f32) * scale

    u = jax.random.uniform(ks[9], (L, RNN_WIDTH), f32, 0.9, 0.999)
    p = u ** (1.0 / LRU_C)
    lru_lambda = jnp.log(p) - jnp.log1p(-p)
    return {
        "x": nrm(ks[0], (BATCH, SEQ, D_MODEL), 1.0),
        "mem": nrm(ks[1], (BATCH, MEM_LEN, D_MODEL), 1.0),
        "w_in": nrm(ks[2], (L, D_MODEL, IN_COLS), D_MODEL ** -0.5),
        "conv_w": nrm(ks[3], (L, CONV_WIDTH, RNN_WIDTH), CONV_WIDTH ** -0.5),
        "conv_b": nrm(ks[4], (L, RNN_WIDTH), 0.01),
        "w_rg": nrm(ks[5], (L, RNN_BLOCKS, RNN_BLOCK, RNN_BLOCK), RNN_BLOCK ** -0.5),
        "b_rg": nrm(ks[6], (L, RNN_WIDTH), 0.01),
        "w_ig": nrm(ks[7], (L, RNN_BLOCKS, RNN_BLOCK, RNN_BLOCK), RNN_BLOCK ** -0.5),
        "b_ig": nrm(ks[8], (L, RNN_WIDTH), 0.01),
        "lru_lambda": lru_lambda,
        "w_br_rnn": nrm(ks[10], (L, RNN_WIDTH, D_MODEL), RNN_WIDTH ** -0.5),
        "w_br_attn": nrm(ks[11], (L, ATTN_WIDTH, D_MODEL), ATTN_WIDTH ** -0.5),
        "sinks": nrm(ks[12], (L, N_Q_HEADS), 0.5),
        "w_out": nrm(ks[13], (L, D_MODEL, D_MODEL), DEEPNORM_BETA * D_MODEL ** -0.5),
        "ln1_g": 1.0 + nrm(ks[14], (L, D_MODEL), 0.02),
        "ln1_b": nrm(ks[15], (L, D_MODEL), 0.02),
        "cq_w": nrm(ks[16], (L, D_MODEL, CROSS_WIDTH), D_MODEL ** -0.5),
        "ckv_w": nrm(ks[17], (L, D_MODEL, 2 * CROSS_WIDTH), D_MODEL ** -0.5),
        "co_w": nrm(ks[18], (L, CROSS_WIDTH, D_MODEL), DEEPNORM_BETA * CROSS_WIDTH ** -0.5),
        "ln2_g": 1.0 + nrm(ks[19], (L, D_MODEL), 0.02),
        "ln2_b": nrm(ks[20], (L, D_MODEL), 0.02),
        "ffn_wi": nrm(ks[21], (L, D_MODEL, 2 * D_FF), D_MODEL ** -0.5),
        "ffn_wo": nrm(ks[22], (L, D_FF, D_MODEL), DEEPNORM_BETA * D_FF ** -0.5),
        "ln3_g": 1.0 + nrm(ks[23], (L, D_MODEL), 0.02),
        "ln3_b": nrm(ks[24], (L, D_MODEL), 0.02),
    }


def reference(x, mem, w_in, conv_w, conv_b, w_rg, b_rg, w_ig, b_ig, lru_lambda,
              w_br_rnn, w_br_attn, sinks, w_out, ln1_g, ln1_b,
              cq_w, ckv_w, co_w, ln2_g, ln2_b,
              ffn_wi, ffn_wo, ln3_g, ln3_b):
    cos, sin = rope_tables(x.shape[1])
    h = x
    for l in range(DEPTH):
        mix = hybrid_mixer(h, w_in[l], conv_w[l], conv_b[l], w_rg[l], b_rg[l], w_ig[l], b_ig[l],
                           lru_lambda[l], w_br_rnn[l], w_br_attn[l], sinks[l], w_out[l], cos, sin)
        h = layer_norm(DEEPNORM_ALPHA * h + mix, ln1_g[l], ln1_b[l])
        h = layer_norm(DEEPNORM_ALPHA * h + cross_attention(h, mem, cq_w[l], ckv_w[l], co_w[l]),
                       ln2_g[l], ln2_b[l])
        h = layer_norm(DEEPNORM_ALPHA * h + swiglu(h, ffn_wi[l], ffn_wo[l]), ln3_g[l], ln3_b[l])
    return h
```

```python
import functools
import math

import jax
import jax.numpy as jnp
from jax import lax
from jax.experimental import pallas as pl
from jax.experimental.pallas import tpu as pltpu

F32 = jnp.float32
BF16 = jnp.bfloat16

DEPTH = 4
RNN_BLOCKS = 4
CONV_WIDTH = 4
LRU_C = 8.0
HEAD_DIM = 64
N_KV_HEADS = 2
WINDOW = 128
ROPE_THETA = 500000.0
ROT_DIM = HEAD_DIM // 4
CROSS_HEADS = 4
LN_EPS = 1e-5
DEEPNORM_ALPHA = (2 * DEPTH) ** 0.25
NEG_INF = -1e30

SUBLANES = 8
LANES = 128
MXU_DIM = 256
VMEM_LIMIT_BYTES = 56 * 1024 * 1024


def _params(n_grid_dims):
    return pltpu.CompilerParams(
        dimension_semantics=("arbitrary",) * n_grid_dims,
        vmem_limit_bytes=VMEM_LIMIT_BYTES,
    )


def _const_spec(shape):
    zeros = (0,) * len(shape)
    return pl.BlockSpec(shape, lambda *_: zeros, pipeline_mode=pl.Buffered(1))


def _layer_norm(x, g, b):
    mu = jnp.mean(x, axis=-1, keepdims=True)
    xc = x - mu
    var = jnp.mean(xc * xc, axis=-1, keepdims=True)
    return xc * lax.rsqrt(var + LN_EPS) * g + b


def _dot(a, b):
    return jnp.dot(a, b, preferred_element_type=F32)


def _dot_nt(a, b):
    return lax.dot_general(a, b, (((1,), (1,)), ((), ())), preferred_element_type=F32)


def _rglru_qkv_kernel(x_ref, w_ref, wgate_ref, convw_ref, convb_ref, brg_ref, big_ref, lam_ref,
                      rc_ref, rs1_ref, rs2_ref,
                      y_ref, q_ref, k_ref, v_ref,
                      tm_s, stage_s, convc_s, hc_s, *, n_batch, ts, d_model, kv_width):
    rows = n_batch * ts
    rnn_block = d_model // RNN_BLOCKS
    tail = (CONV_WIDTH - 1) * n_batch

    @pl.when(pl.program_id(0) == 0)
    def _():
        convc_s[...] = jnp.zeros_like(convc_s)
        hc_s[...] = jnp.zeros_like(hc_s)

    xs = jnp.concatenate([x_ref[b] for b in range(n_batch)], axis=0).astype(BF16)

    def to_time_major(val):
        n_blk = val.shape[1] // LANES
        for j in range(n_blk):
            tm_s[j] = val[:, j * LANES:(j + 1) * LANES]
        return jnp.concatenate(
            [jnp.concatenate([tm_s[j, pl.ds(t, n_batch, stride=ts), :] for j in range(n_blk)], axis=1)
             for t in range(ts)], axis=0)

    neg_c_softplus = -LRU_C * jax.nn.softplus(-lam_ref[...])

    for n in range(RNN_BLOCKS):
        cs = slice(n * rnn_block, (n + 1) * rnn_block)
        xr = to_time_major(_dot(xs, w_ref[:, cs]))
        hist = convc_s[:, cs]
        ext = jnp.concatenate([hist, xr], axis=0)
        convc_s[:, cs] = xr[rows - tail:, :]
        xc = convb_ref[:, cs]
        for kk in range(CONV_WIDTH):
            xc = xc + ext[kk * n_batch:kk * n_batch + rows, :] * convw_ref[kk:kk + 1, cs]
        gates = _dot(xc.astype(BF16), wgate_ref[n])
        r = jax.nn.sigmoid(gates[:, :rnn_block] + brg_ref[:, cs])
        i = jax.nn.sigmoid(gates[:, rnn_block:] + big_ref[:, cs])
        log_a = neg_c_softplus[:, cs] * r
        a = jnp.exp(log_a)
        mult = jnp.sqrt(-jnp.tanh(log_a) * (a * a + 1.0))
        b_in = mult * (i * xc)
        h = hc_s[:, cs]
        hs = []
        for t in range(ts):
            h = a[t * n_batch:(t + 1) * n_batch, :] * h + b_in[t * n_batch:(t + 1) * n_batch, :]
            hs.append(h)
        hc_s[:, cs] = h
        hseq = jnp.concatenate(hs, axis=0)
        gr = to_time_major(
            _dot(xs, w_ref[:, d_model + n * rnn_block:d_model + (n + 1) * rnn_block]))
        y = hseq * jax.nn.gelu(gr)
        for j in range(rnn_block // LANES):
            stage_s[n * (rnn_block // LANES) + j] = y[:, j * LANES:(j + 1) * LANES]
    for b in range(n_batch):
        y_ref[b] = jnp.concatenate(
            [stage_s[j, pl.ds(b, ts, stride=n_batch), :] for j in range(d_model // LANES)],
            axis=1).astype(BF16)

    rc, rs1, rs2 = rc_ref[...], rs1_ref[...], rs2_ref[...]
    half = ROT_DIM // 2

    def rope(tile):
        return (tile * rc + pltpu.roll(tile, LANES - half, axis=1) * rs1
                + pltpu.roll(tile, half, axis=1) * rs2)

    q_off = 2 * d_model
    scale = HEAD_DIM ** -0.5
    for c in range(d_model // MXU_DIM):
        qc = _dot(xs, w_ref[:, q_off + c * MXU_DIM:q_off + (c + 1) * MXU_DIM])
        for b in range(n_batch):
            qb = qc[b * ts:(b + 1) * ts, :]
            parts = [rope(qb[:, j * LANES:(j + 1) * LANES]) for j in range(MXU_DIM // LANES)]
            q_ref[b, :, c * MXU_DIM:(c + 1) * MXU_DIM] = (
                jnp.concatenate(parts, axis=1) * scale).astype(BF16)

    kv_off = 3 * d_model
    kv = _dot(xs, w_ref[:, kv_off:kv_off + 2 * kv_width])
    for b in range(n_batch):
        kvb = kv[b * ts:(b + 1) * ts, :]
        k_ref[b] = rope(kvb[:, :kv_width]).astype(BF16)
        v_ref[b] = kvb[:, kv_width:].astype(BF16)


def _rope_tables(seq_len):
    half = ROT_DIM // 2
    pos = jnp.arange(seq_len, dtype=F32)
    inv_freq = ROPE_THETA ** (-jnp.arange(0, ROT_DIM, 2, dtype=F32) / ROT_DIM)
    ang = pos[:, None] * inv_freq[None, :]
    cos, sin = jnp.cos(ang), jnp.sin(ang)
    ones = jnp.ones((seq_len, HEAD_DIM - ROT_DIM), F32)
    zeros = jnp.zeros((seq_len, HEAD_DIM - ROT_DIM), F32)
    zh = jnp.zeros((seq_len, half), F32)
    c_head = jnp.concatenate([cos, cos, ones], axis=1)
    s1_head = jnp.concatenate([-sin, zh, zeros], axis=1)
    s2_head = jnp.concatenate([zh, sin, zeros], axis=1)
    reps = LANES // HEAD_DIM
    return (jnp.tile(c_head, (1, reps)), jnp.tile(s1_head, (1, reps)), jnp.tile(s2_head, (1, reps)))


def _rglru_qkv(h, w_a, w_gate, conv_w, conv_b, b_rg, b_ig, lam, rope_tabs, *, ts):
    n_batch, seq, d_model = h.shape
    kv_width = N_KV_HEADS * HEAD_DIM
    assert n_batch == SUBLANES and seq % ts == 0 and ts % SUBLANES == 0 and kv_width == LANES
    rows = n_batch * ts
    n_cols = w_a.shape[1]
    rnn_block = d_model // RNN_BLOCKS
    kernel = functools.partial(_rglru_qkv_kernel, n_batch=n_batch, ts=ts, d_model=d_model,
                               kv_width=kv_width)
    tile3 = lambda width: pl.BlockSpec((n_batch, ts, width), lambda i: (0, i, 0))
    rope_spec = pl.BlockSpec((ts, LANES), lambda i: (i, 0))
    return pl.pallas_call(
        kernel,
        grid=(seq // ts,),
        in_specs=[
            tile3(d_model),
            _const_spec((d_model, n_cols)),
            _const_spec((RNN_BLOCKS, rnn_block, 2 * rnn_block)),
            _const_spec((CONV_WIDTH, d_model)),
            _const_spec((1, d_model)), _const_spec((1, d_model)), _const_spec((1, d_model)),
            _const_spec((1, d_model)),
            rope_spec, rope_spec, rope_spec,
        ],
        out_specs=[tile3(d_model), tile3(d_model), tile3(kv_width), tile3(kv_width)],
        out_shape=[
            jax.ShapeDtypeStruct((n_batch, seq, d_model), BF16),
            jax.ShapeDtypeStruct((n_batch, seq, d_model), BF16),
            jax.ShapeDtypeStruct((n_batch, seq, kv_width), BF16),
            jax.ShapeDtypeStruct((n_batch, seq, kv_width), BF16),
        ],
        scratch_shapes=[
            pltpu.VMEM((rnn_block // LANES, rows, LANES), F32),
            pltpu.VMEM((d_model // LANES, rows, LANES), F32),
            pltpu.VMEM(((CONV_WIDTH - 1) * n_batch, d_model), F32),
            pltpu.VMEM((n_batch, d_model), F32),
        ],
        compiler_params=_params(1),
        name="rglru_qkv",
    )(h, w_a, w_gate, conv_w, conv_b, b_rg, b_ig, lam, *rope_tabs)


def _swa_kernel(sinks_ref, q_ref, kp_ref, kc_ref, vp_ref, vc_ref, o_ref, *, n_q_heads):
    blk = q_ref.shape[0]
    group = n_q_heads // N_KV_HEADS
    n = pl.program_id(1)
    keys = jnp.concatenate([kp_ref[...], kc_ref[...]], axis=0)
    vals = jnp.concatenate([vp_ref[...], vc_ref[...]], axis=0)
    row = lax.broadcasted_iota(jnp.int32, (blk, 2 * blk), 0)
    col = lax.broadcasted_iota(jnp.int32, (blk, 2 * blk), 1)
    first_prev = jnp.where(n > 0, 0, blk)
    valid = ((col > row) & (col < blk) & (col >= first_prev)) | ((col >= blk) & (col - blk <= row))
    outs = []
    for g in range(N_KV_HEADS):
        kg = keys[:, g * HEAD_DIM:(g + 1) * HEAD_DIM]
        vg = vals[:, g * HEAD_DIM:(g + 1) * HEAD_DIM]
        for hh in range(group):
            head = g * group + hh
            qh = q_ref[:, head * HEAD_DIM:(head + 1) * HEAD_DIM]
            s = jnp.where(valid, _dot_nt(qh, kg), NEG_INF)
            sink = sinks_ref[head]
            m = jnp.maximum(jnp.max(s, axis=-1, keepdims=True), sink)
            p = jnp.exp(s - m)
            denom = jnp.sum(p, axis=-1, keepdims=True) + jnp.exp(sink - m)
            outs.append(_dot(p.astype(BF16), vg) / denom)
    o_ref[...] = jnp.concatenate(outs, axis=1).astype(o_ref.dtype)


def _swa(q, k, v, sinks):
    n_batch, seq, width = q.shape
    kv_width = k.shape[-1]
    blk = WINDOW
    n_q_heads = width // HEAD_DIM
    prev = lambda b, n: (b, jnp.maximum(n - 1, 0), 0)
    cur = lambda b, n: (b, n, 0)
    kv_spec = lambda imap: pl.BlockSpec((None, blk, kv_width), imap)
    return pl.pallas_call(
        functools.partial(_swa_kernel, n_q_heads=n_q_heads),
        grid=(n_batch, seq // blk),
        in_specs=[
            pl.BlockSpec(memory_space=pltpu.SMEM),
            pl.BlockSpec((None, blk, width), cur),
            kv_spec(prev), kv_spec(cur), kv_spec(prev), kv_spec(cur),
        ],
        out_specs=pl.BlockSpec((None, blk, width), cur),
        out_shape=jax.ShapeDtypeStruct((n_batch, seq, width), BF16),
        compiler_params=_params(2),
        name="swa",
    )(sinks, q, k, k, v, v)


def _merge_ln_kernel(h_ref, yr_ref, ya_ref, wg_ref, wbr_ref, wba_ref, wo_ref, g_ref, b_ref, o_ref,
                     merged_s):
    d_model = h_ref.shape[1]
    h = h_ref[...]
    hb = h.astype(BF16)
    yr = yr_ref[...]
    ya = ya_ref[...]
    for c in range(d_model // MXU_DIM):
        cs = slice(c * MXU_DIM, (c + 1) * MXU_DIM)
        cs2 = slice(d_model + c * MXU_DIM, d_model + (c + 1) * MXU_DIM)
        gate_r = jax.nn.sigmoid(_dot(hb, wg_ref[:, cs]))
        gate_a = jax.nn.sigmoid(_dot(hb, wg_ref[:, cs2]))
        merged = gate_r * _dot(yr, wbr_ref[:, cs]) + gate_a * _dot(ya, wba_ref[:, cs])
        merged_s[:, cs] = merged.astype(BF16)
    mix = _dot(merged_s[...], wo_ref[...])
    o_ref[...] = _layer_norm(DEEPNORM_ALPHA * h + mix, g_ref[...], b_ref[...])


def _merge_ln(h2, yr2, ya2, w_g, w_br, w_ba, w_o, ln_g, ln_b, *, tm):
    n_tok, d_model = h2.shape
    tile = lambda: pl.BlockSpec((tm, d_model), lambda i: (i, 0))
    return pl.pallas_call(
        _merge_ln_kernel,
        grid=(n_tok // tm,),
        in_specs=[tile(), tile(), tile(),
                  _const_spec(w_g.shape), _const_spec(w_br.shape), _const_spec(w_ba.shape),
                  _const_spec(w_o.shape), _const_spec((1, d_model)), _const_spec((1, d_model))],
        out_specs=tile(),
        out_shape=jax.ShapeDtypeStruct((n_tok, d_model), F32),
        scratch_shapes=[pltpu.VMEM((tm, d_model), BF16)],
        compiler_params=_params(1),
        name="merge_ln",
    )(h2, yr2, ya2, w_g, w_br, w_ba, w_o, ln_g, ln_b)


def _cross_ln_kernel(h_ref, mem_ref, wq_ref, wkv_ref, wo_ref, g_ref, b_ref, o_ref, kv_s, att_s):
    d_model = h_ref.shape[1]
    head_dim = d_model // CROSS_HEADS

    @pl.when(pl.program_id(1) == 0)
    def _():
        kv_s[...] = _dot(mem_ref[...].astype(BF16), wkv_ref[...]).astype(BF16)

    h = h_ref[...]
    hb = h.astype(BF16)
    scale = head_dim ** -0.5
    for hd in range(CROSS_HEADS):
        cs = slice(hd * head_dim, (hd + 1) * head_dim)
        q = (_dot(hb, wq_ref[:, cs]) * scale).astype(BF16)
        s = _dot_nt(q, kv_s[:, cs])
        m = jnp.max(s, axis=-1, keepdims=True)
        p = jnp.exp(s - m)
        denom = jnp.sum(p, axis=-1, keepdims=True)
        v = kv_s[:, d_model + hd * head_dim:d_model + (hd + 1) * head_dim]
        att_s[:, cs] = (_dot(p.astype(BF16), v) / denom).astype(BF16)
    out = _dot(att_s[...], wo_ref[...])
    o_ref[...] = _layer_norm(DEEPNORM_ALPHA * h + out, g_ref[...], b_ref[...])


def _cross_ln(h, mem, w_q, w_kv, w_o, ln_g, ln_b, *, tm):
    n_batch, seq, d_model = h.shape
    mem_len = mem.shape[1]
    tile = pl.BlockSpec((None, tm, d_model), lambda b, i: (b, i, 0))
    return pl.pallas_call(
        _cross_ln_kernel,
        grid=(n_batch, seq // tm),
        in_specs=[tile,
                  pl.BlockSpec((None, mem_len, d_model), lambda b, i: (b, 0, 0)),
                  _const_spec(w_q.shape), _const_spec(w_kv.shape), _const_spec(w_o.shape),
                  _const_spec((1, d_model)), _const_spec((1, d_model))],
        out_specs=tile,
        out_shape=jax.ShapeDtypeStruct((n_batch, seq, d_model), F32),
        scratch_shapes=[pltpu.VMEM((mem_len, 2 * d_model), BF16), pltpu.VMEM((tm, d_model), BF16)],
        compiler_params=_params(2),
        name="cross_ln",
    )(h, mem, w_q, w_kv, w_o, ln_g, ln_b)


def _ffn_ln_kernel(h_ref, wi_ref, wo_ref, g_ref, b_ref, o_ref, act_s):
    d_ff = wo_ref.shape[0]
    h = h_ref[...]
    hb = h.astype(BF16)
    for c in range(d_ff // MXU_DIM):
        cs = slice(c * MXU_DIM, (c + 1) * MXU_DIM)
        cs_up = slice(d_ff + c * MXU_DIM, d_ff + (c + 1) * MXU_DIM)
        gate = _dot(hb, wi_ref[:, cs])
        up = _dot(hb, wi_ref[:, cs_up])
        act_s[:, cs] = (jax.nn.silu(gate) * up).astype(BF16)
    out = _dot(act_s[...], wo_ref[...])
    o_ref[...] = _layer_norm(DEEPNORM_ALPHA * h + out, g_ref[...], b_ref[...])


def _ffn_ln(h2, w_i, w_o, ln_g, ln_b, *, tm):
    n_tok, d_model = h2.shape
    d_ff = w_o.shape[0]
    assert d_ff % MXU_DIM == 0
    tile = lambda: pl.BlockSpec((tm, d_model), lambda i: (i, 0))
    return pl.pallas_call(
        _ffn_ln_kernel,
        grid=(n_tok // tm,),
        in_specs=[tile(), _const_spec(w_i.shape), _const_spec(w_o.shape),
                  _const_spec((1, d_model)), _const_spec((1, d_model))],
        out_specs=tile(),
        out_shape=jax.ShapeDtypeStruct((n_tok, d_model), F32),
        scratch_shapes=[pltpu.VMEM((tm, d_ff), BF16)],
        compiler_params=_params(1),
        name="ffn_ln",
    )(h2, w_i, w_o, ln_g, ln_b)


def _tile_sizes(seq):
    ts = 64 if seq % 64 == 0 else seq
    tm = 512 if seq % 512 == 0 else seq
    return ts, tm


def kernel(x, mem, w_in, conv_w, conv_b, w_rg, b_rg, w_ig, b_ig, lru_lambda, w_br_rnn, w_br_attn, sinks, w_out, ln1_g, ln1_b, cq_w, ckv_w, co_w, ln2_g, ln2_b, ffn_wi, ffn_wo, ln3_g, ln3_b):
    n_batch, seq, d_model = x.shape
    depth = w_in.shape[0]
    n_tok = n_batch * seq
    kv_width = N_KV_HEADS * HEAD_DIM
    n_a = 3 * d_model + 2 * kv_width
    ts, tm = _tile_sizes(seq)
    rope_tabs = _rope_tables(seq)
    row = lambda p: p.reshape(1, -1)

    h = x
    for l in range(depth):
        w_a = w_in[l, :, :n_a].astype(BF16)
        w_g = w_in[l, :, n_a:].astype(BF16)
        w_gate = jnp.concatenate([w_rg[l], w_ig[l]], axis=-1).astype(BF16)
        y_rnn, q, k, v = _rglru_qkv(h, w_a, w_gate, conv_w[l], row(conv_b[l]), row(b_rg[l]),
                                    row(b_ig[l]), row(lru_lambda[l]), rope_tabs, ts=ts)
        y_attn = _swa(q, k, v, sinks[l])
        h2 = _merge_ln(h.reshape(n_tok, d_model), y_rnn.reshape(n_tok, d_model),
                       y_attn.reshape(n_tok, d_model), w_g, w_br_rnn[l].astype(BF16),
                       w_br_attn[l].astype(BF16), w_out[l].astype(BF16),
                       row(ln1_g[l]), row(ln1_b[l]), tm=tm)
        h = _cross_ln(h2.reshape(n_batch, seq, d_model), mem, cq_w[l].astype(BF16),
                      ckv_w[l].astype(BF16), co_w[l].astype(BF16), row(ln2_g[l]), row(ln2_b[l]),
                      tm=tm)
        h2 = _ffn_ln(h.reshape(n_tok, d_model), ffn_wi[l].astype(BF16), ffn_wo[l].astype(BF16),
                     row(ln3_g[l]), row(ln3_b[l]), tm=tm)
        h = h2.reshape(n_batch, seq, d_model)
    return h
```

```python
import functools
import math

import jax
import jax.numpy as jnp
from jax import lax
from jax.experimental import pallas as pl
from jax.experimental.pallas import tpu as pltpu

F32 = jnp.float32
BF16 = jnp.bfloat16

DEPTH = 4
RNN_BLOCKS = 4
CONV_WIDTH = 4
LRU_C = 8.0
HEAD_DIM = 64
N_KV_HEADS = 2
WINDOW = 128
ROPE_THETA = 500000.0
ROT_DIM = HEAD_DIM // 4
CROSS_HEADS = 4
LN_EPS = 1e-5
DEEPNORM_ALPHA = (2 * DEPTH) ** 0.25
NEG_INF = -1e30

SUBLANES = 8
LANES = 128
MXU_DIM = 256
VMEM_LIMIT_BYTES = 56 * 1024 * 1024


def _params(n_grid_dims):
    return pltpu.CompilerParams(
        dimension_semantics=("arbitrary",) * n_grid_dims,
        vmem_limit_bytes=VMEM_LIMIT_BYTES,
    )


def _const_spec(shape):
    zeros = (0,) * len(shape)
    return pl.BlockSpec(shape, lambda *_: zeros, pipeline_mode=pl.Buffered(1))


def _layer_norm(x, g, b):
    mu = jnp.mean(x, axis=-1, keepdims=True)
    xc = x - mu
    var = jnp.mean(xc * xc, axis=-1, keepdims=True)
    return xc * lax.rsqrt(var + LN_EPS) * g + b


def _sigmoid(x):
    return 0.5 * jnp.tanh(0.5 * x) + 0.5


def _sqrt_nonneg(x):
    return jnp.where(x > 0.0, x * lax.rsqrt(x), 0.0)


def _dot(a, b):
    return jnp.dot(a, b, preferred_element_type=F32)


def _dot_nt(a, b):
    return lax.dot_general(a, b, (((1,), (1,)), ((), ())), preferred_element_type=F32)


def _rglru_qkv_kernel(x_ref, w_ref, wgate_ref, convw_ref, convb_ref, brg_ref, big_ref, lam_ref,
                      rc_ref, rs1_ref, rs2_ref,
                      y_ref, q_ref, kv_ref,
                      tm_s, stage_s, ab_s, convc_s, hc_s, *, n_batch, ts, d_model, kv_width):
    rows = n_batch * ts
    rnn_block = d_model // RNN_BLOCKS
    tail = (CONV_WIDTH - 1) * n_batch
    pitch = _batch_pitch(ts)

    @pl.when(pl.program_id(0) == 0)
    def _():
        convc_s[...] = jnp.zeros_like(convc_s)
        hc_s[...] = jnp.zeros_like(hc_s)

    xs = jnp.concatenate([x_ref[b] for b in range(n_batch)], axis=0).astype(BF16)

    def to_time_major(val, slot):
        n_blk = val.shape[1] // LANES
        for j in range(n_blk):
            for b in range(n_batch):
                tm_s[slot * n_blk + j, b * pitch:b * pitch + ts, :] = (
                    val[b * ts:(b + 1) * ts, j * LANES:(j + 1) * LANES])
        return jnp.concatenate(
            [jnp.concatenate([tm_s[slot * n_blk + j, pl.ds(t, n_batch, stride=pitch), :]
                              for j in range(n_blk)], axis=1)
             for t in range(ts)], axis=0)

    rc, rs1, rs2 = rc_ref[...], rs1_ref[...], rs2_ref[...]
    half = ROT_DIM // 2

    def rope(tile, tabs):
        c, s1, s2 = tabs
        return (tile * c + pltpu.roll(tile, LANES - half, axis=1) * s1
                + pltpu.roll(tile, half, axis=1) * s2)

    q_off = 2 * d_model
    scale = HEAD_DIM ** -0.5
    k_tabs = (rc, rs1, rs2)
    q_tabs = (rc * scale, rs1 * scale, rs2 * scale)

    def emit_q(c):
        qc = _dot(xs, w_ref[:, q_off + c * MXU_DIM:q_off + (c + 1) * MXU_DIM])
        for b in range(n_batch):
            qb = qc[b * ts:(b + 1) * ts, :]
            parts = [rope(qb[:, j * LANES:(j + 1) * LANES], q_tabs) for j in range(MXU_DIM // LANES)]
            q_ref[b, :, c * MXU_DIM:(c + 1) * MXU_DIM] = jnp.concatenate(parts, axis=1).astype(BF16)

    def emit_kv():
        kv_off = 3 * d_model
        kv = _dot(xs, w_ref[:, kv_off:kv_off + 2 * kv_width])
        for b in range(n_batch):
            kvb = kv[b * ts:(b + 1) * ts, :]
            kb = rope(kvb[:, :kv_width], k_tabs)
            vb = kvb[:, kv_width:]
            kv_ref[b] = jnp.concatenate(
                [kb, pltpu.roll(kb, HEAD_DIM, axis=1), vb, pltpu.roll(vb, HEAD_DIM, axis=1)],
                axis=1).astype(BF16)

    neg_c_softplus = -LRU_C * jax.nn.softplus(-lam_ref[...])

    for n in range(RNN_BLOCKS):
        cs = slice(n * rnn_block, (n + 1) * rnn_block)
        xr = to_time_major(_dot(xs, w_ref[:, cs]), n)
        hist = convc_s[:, cs]
        ext = jnp.concatenate([hist, xr], axis=0)
        convc_s[:, cs] = xr[rows - tail:, :]
        xc = convb_ref[:, cs]
        for kk in range(CONV_WIDTH):
            xc = xc + ext[kk * n_batch:kk * n_batch + rows, :] * convw_ref[kk:kk + 1, cs]
        gates = _dot(xc.astype(BF16), wgate_ref[n])
        emit_q(n)
        r = _sigmoid(gates[:, :rnn_block] + brg_ref[:, cs])
        i = _sigmoid(gates[:, rnn_block:] + big_ref[:, cs])
        log_a = neg_c_softplus[:, cs] * r
        a = jnp.exp(log_a)
        mult = _sqrt_nonneg(-jnp.tanh(log_a) * (a * a + 1.0))
        ab_s[0, :, cs] = a
        ab_s[1, :, cs] = mult * (i * xc)

    h = hc_s[...]
    for t in range(ts):
        rs = slice(t * n_batch, (t + 1) * n_batch)
        h = ab_s[0, rs, :] * h + ab_s[1, rs, :]
        for j in range(d_model // LANES):
            stage_s[j, rs, :] = h[:, j * LANES:(j + 1) * LANES]
    hc_s[...] = h

    blk_lanes = rnn_block // LANES
    for n in range(RNN_BLOCKS):
        gate = jax.nn.gelu(
            _dot(xs, w_ref[:, d_model + n * rnn_block:d_model + (n + 1) * rnn_block]))
        for b in range(n_batch):
            hb = jnp.concatenate(
                [stage_s[n * blk_lanes + j, pl.ds(b, ts, stride=n_batch), :] for j in range(blk_lanes)],
                axis=1)
            y_ref[b, :, n * rnn_block:(n + 1) * rnn_block] = (
                hb * gate[b * ts:(b + 1) * ts, :]).astype(BF16)
    emit_kv()


def _batch_pitch(ts):
    groups = ts // SUBLANES
    return SUBLANES * (groups + 1 if groups % 2 == 0 else groups)


def _rope_tables(seq_len):
    half = ROT_DIM // 2
    pos = jnp.arange(seq_len, dtype=F32)
    inv_freq = ROPE_THETA ** (-jnp.arange(0, ROT_DIM, 2, dtype=F32) / ROT_DIM)
    ang = pos[:, None] * inv_freq[None, :]
    cos, sin = jnp.cos(ang), jnp.sin(ang)
    ones = jnp.ones((seq_len, HEAD_DIM - ROT_DIM), F32)
    zeros = jnp.zeros((seq_len, HEAD_DIM - ROT_DIM), F32)
    zh = jnp.zeros((seq_len, half), F32)
    c_head = jnp.concatenate([cos, cos, ones], axis=1)
    s1_head = jnp.concatenate([-sin, zh, zeros], axis=1)
    s2_head = jnp.concatenate([zh, sin, zeros], axis=1)
    reps = LANES // HEAD_DIM
    return (jnp.tile(c_head, (1, reps)), jnp.tile(s1_head, (1, reps)), jnp.tile(s2_head, (1, reps)))


def _rglru_qkv(h, w_a, w_gate, conv_w, conv_b, b_rg, b_ig, lam, rope_tabs, *, ts):
    n_batch, seq, d_model = h.shape
    kv_width = N_KV_HEADS * HEAD_DIM
    assert n_batch == SUBLANES and seq % ts == 0 and ts % SUBLANES == 0 and kv_width == LANES
    rows = n_batch * ts
    n_cols = w_a.shape[1]
    rnn_block = d_model // RNN_BLOCKS
    kernel = functools.partial(_rglru_qkv_kernel, n_batch=n_batch, ts=ts, d_model=d_model,
                               kv_width=kv_width)
    tile3 = lambda width: pl.BlockSpec((n_batch, ts, width), lambda i: (0, i, 0))
    rope_spec = pl.BlockSpec((ts, LANES), lambda i: (i, 0))
    return pl.pallas_call(
        kernel,
        grid=(seq // ts,),
        in_specs=[
            tile3(d_model),
            _const_spec((d_model, n_cols)),
            _const_spec((RNN_BLOCKS, rnn_block, 2 * rnn_block)),
            _const_spec((CONV_WIDTH, d_model)),
            _const_spec((1, d_model)), _const_spec((1, d_model)), _const_spec((1, d_model)),
            _const_spec((1, d_model)),
            rope_spec, rope_spec, rope_spec,
        ],
        out_specs=[tile3(d_model), tile3(d_model), tile3(4 * kv_width)],
        out_shape=[
            jax.ShapeDtypeStruct((n_batch, seq, d_model), BF16),
            jax.ShapeDtypeStruct((n_batch, seq, d_model), BF16),
            jax.ShapeDtypeStruct((n_batch, seq, 4 * kv_width), BF16),
        ],
        scratch_shapes=[
            pltpu.VMEM((d_model // LANES, n_batch * _batch_pitch(ts), LANES), F32),
            pltpu.VMEM((d_model // LANES, rows, LANES), F32),
            pltpu.VMEM((2, rows, d_model), F32),
            pltpu.VMEM(((CONV_WIDTH - 1) * n_batch, d_model), F32),
            pltpu.VMEM((n_batch, d_model), F32),
        ],
        compiler_params=_params(1),
        name="rglru_qkv",
    )(h, w_a, w_gate, conv_w, conv_b, b_rg, b_ig, lam, *rope_tabs)


def _swa_kernel(sinks_ref, q_ref, kvp_ref, kvc_ref, o_ref, *, n_q_heads):
    blk = q_ref.shape[0]
    group = n_q_heads // N_KV_HEADS
    pairs = group // 2
    n = pl.program_id(1)
    row = lax.broadcasted_iota(jnp.int32, (blk, blk), 0)
    col = lax.broadcasted_iota(jnp.int32, (blk, blk), 1)
    upper = col > row
    no_prev = jnp.where(n == 0, NEG_INF, 0.0)
    bias = jnp.where(upper, no_prev, 0.0)
    low_lanes = lax.broadcasted_iota(jnp.int32, (blk, LANES), 1) < HEAD_DIM

    def keep(x, low):
        return jnp.where(low_lanes if low else ~low_lanes, x, jnp.zeros_like(x))

    def lanes(ref, j):
        return ref[:, j * LANES:(j + 1) * LANES]

    for g in range(N_KV_HEADS):
        low_src, high_src = (0, 1) if g == 0 else (1, 0)
        kk = jnp.concatenate([keep(lanes(kvp_ref, low_src), True), keep(lanes(kvc_ref, low_src), True),
                              keep(lanes(kvp_ref, high_src), False), keep(lanes(kvc_ref, high_src), False)],
                             axis=0)
        vv = jnp.concatenate([keep(lanes(kvp_ref, 2 + low_src), True), keep(lanes(kvc_ref, 2 + low_src), True),
                              keep(lanes(kvp_ref, 2 + high_src), False), keep(lanes(kvc_ref, 2 + high_src), False)],
                             axis=0)
        qg = jnp.concatenate([lanes(q_ref, g * pairs + p) for p in range(pairs)], axis=0)
        s = _dot_nt(qg, kk)
        probs, recips = [], []
        for p in range(pairs):
            sp = s[p * blk:(p + 1) * blk, :]
            row_probs = []
            for half in range(2):
                sink = sinks_ref[g * group + 2 * p + half]
                comb = jnp.where(upper, sp[:, 2 * half * blk:(2 * half + 1) * blk],
                                 sp[:, (2 * half + 1) * blk:(2 * half + 2) * blk]) + bias
                m = jnp.maximum(jnp.max(comb, axis=-1, keepdims=True), sink)
                e = jnp.exp(comb - m)
                denom = jnp.sum(e, axis=-1, keepdims=True) + jnp.exp(sink - m)
                recips.append(1.0 / denom)
                row_probs += [jnp.where(upper, e, 0.0).astype(BF16), jnp.where(upper, 0.0, e).astype(BF16)]
            probs.append(jnp.concatenate(row_probs, axis=1))
        o = _dot(jnp.concatenate(probs, axis=0), vv)
        for p in range(pairs):
            scale = jnp.where(low_lanes, recips[2 * p], recips[2 * p + 1])
            j = g * pairs + p
            o_ref[:, j * LANES:(j + 1) * LANES] = (o[p * blk:(p + 1) * blk, :] * scale).astype(o_ref.dtype)


def _swa(q, kv, sinks):
    n_batch, seq, width = q.shape
    blk = WINDOW
    n_q_heads = width // HEAD_DIM
    assert 2 * HEAD_DIM == LANES and N_KV_HEADS == 2 and kv.shape[-1] == 4 * LANES
    prev = lambda b, n: (b, jnp.maximum(n - 1, 0), 0)
    cur = lambda b, n: (b, n, 0)
    return pl.pallas_call(
        functools.partial(_swa_kernel, n_q_heads=n_q_heads),
        grid=(n_batch, seq // blk),
        in_specs=[
            pl.BlockSpec(memory_space=pltpu.SMEM),
            pl.BlockSpec((None, blk, width), cur),
            pl.BlockSpec((None, blk, kv.shape[-1]), prev),
            pl.BlockSpec((None, blk, kv.shape[-1]), cur),
        ],
        out_specs=pl.BlockSpec((None, blk, width), cur),
        out_shape=jax.ShapeDtypeStruct((n_batch, seq, width), BF16),
        compiler_params=_params(2),
        name="swa",
    )(sinks, q, kv, kv)


def _merge_ln_kernel(h_ref, yr_ref, ya_ref, wg_ref, wbr_ref, wba_ref, wo_ref, g_ref, b_ref, o_ref,
                     merged_s):
    d_model = h_ref.shape[1]
    h = h_ref[...]
    hb = h.astype(BF16)
    yr = yr_ref[...]
    ya = ya_ref[...]
    for c in range(d_model // MXU_DIM):
        cs = slice(c * MXU_DIM, (c + 1) * MXU_DIM)
        cs2 = slice(d_model + c * MXU_DIM, d_model + (c + 1) * MXU_DIM)
        gate_r = _sigmoid(_dot(hb, wg_ref[:, cs]))
        gate_a = _sigmoid(_dot(hb, wg_ref[:, cs2]))
        merged = gate_r * _dot(yr, wbr_ref[:, cs]) + gate_a * _dot(ya, wba_ref[:, cs])
        merged_s[:, cs] = merged.astype(BF16)
    mix = _dot(merged_s[...], wo_ref[...])
    o_ref[...] = _layer_norm(DEEPNORM_ALPHA * h + mix, g_ref[...], b_ref[...])


def _merge_ln(h2, yr2, ya2, w_g, w_br, w_ba, w_o, ln_g, ln_b, *, tm):
    n_tok, d_model = h2.shape
    tile = lambda: pl.BlockSpec((tm, d_model), lambda i: (i, 0))
    return pl.pallas_call(
        _merge_ln_kernel,
        grid=(n_tok // tm,),
        in_specs=[tile(), tile(), tile(),
                  _const_spec(w_g.shape), _const_spec(w_br.shape), _const_spec(w_ba.shape),
                  _const_spec(w_o.shape), _const_spec((1, d_model)), _const_spec((1, d_model))],
        out_specs=tile(),
        out_shape=jax.ShapeDtypeStruct((n_tok, d_model), F32),
        scratch_shapes=[pltpu.VMEM((tm, d_model), BF16)],
        compiler_params=_params(1),
        name="merge_ln",
    )(h2, yr2, ya2, w_g, w_br, w_ba, w_o, ln_g, ln_b)


def _cross_ln_kernel(h_ref, mem_ref, wq_ref, wkv_ref, wo_ref, g_ref, b_ref, o_ref, kv_s, att_s):
    d_model = h_ref.shape[1]
    head_dim = d_model // CROSS_HEADS

    @pl.when(pl.program_id(1) == 0)
    def _():
        kv_s[...] = _dot(mem_ref[...].astype(BF16), wkv_ref[...]).astype(BF16)

    h = h_ref[...]
    hb = h.astype(BF16)
    scale = head_dim ** -0.5
    for hd in range(CROSS_HEADS):
        cs = slice(hd * head_dim, (hd + 1) * head_dim)
        q = (_dot(hb, wq_ref[:, cs]) * scale).astype(BF16)
        s = _dot_nt(q, kv_s[:, cs])
        m = jnp.max(s, axis=-1, keepdims=True)
        p = jnp.exp(s - m)
        denom = jnp.sum(p, axis=-1, keepdims=True)
        v = kv_s[:, d_model + hd * head_dim:d_model + (hd + 1) * head_dim]
        att_s[:, cs] = (_dot(p.astype(BF16), v) / denom).astype(BF16)
    out = _dot(att_s[...], wo_ref[...])
    o_ref[...] = _layer_norm(DEEPNORM_ALPHA * h + out, g_ref[...], b_ref[...])


def _cross_ln(h, mem, w_q, w_kv, w_o, ln_g, ln_b, *, tm):
    n_batch, seq, d_model = h.shape
    mem_len = mem.shape[1]
    tile = pl.BlockSpec((None, tm, d_model), lambda b, i: (b, i, 0))
    return pl.pallas_call(
        _cross_ln_kernel,
        grid=(n_batch, seq // tm),
        in_specs=[tile,
                  pl.BlockSpec((None, mem_len, d_model), lambda b, i: (b, 0, 0)),
                  _const_spec(w_q.shape), _const_spec(w_kv.shape), _const_spec(w_o.shape),
                  _const_spec((1, d_model)), _const_spec((1, d_model))],
        out_specs=tile,
        out_shape=jax.ShapeDtypeStruct((n_batch, seq, d_model), F32),
        scratch_shapes=[pltpu.VMEM((mem_len, 2 * d_model), BF16), pltpu.VMEM((tm, d_model), BF16)],
        compiler_params=_params(2),
        name="cross_ln",
    )(h, mem, w_q, w_kv, w_o, ln_g, ln_b)


def _ffn_ln_kernel(h_ref, wi_ref, wo_ref, g_ref, b_ref, o_ref, act_s):
    d_ff = wo_ref.shape[0]
    h = h_ref[...]
    hb = h.astype(BF16)
    for c in range(d_ff // MXU_DIM):
        cs = slice(c * MXU_DIM, (c + 1) * MXU_DIM)
        cs_up = slice(d_ff + c * MXU_DIM, d_ff + (c + 1) * MXU_DIM)
        gate = _dot(hb, wi_ref[:, cs])
        up = _dot(hb, wi_ref[:, cs_up])
        act_s[:, cs] = (jax.nn.silu(gate) * up).astype(BF16)
    out = _dot(act_s[...], wo_ref[...])
    o_ref[...] = _layer_norm(DEEPNORM_ALPHA * h + out, g_ref[...], b_ref[...])


def _ffn_ln(h2, w_i, w_o, ln_g, ln_b, *, tm):
    n_tok, d_model = h2.shape
    d_ff = w_o.shape[0]
    assert d_ff % MXU_DIM == 0
    tile = lambda: pl.BlockSpec((tm, d_model), lambda i: (i, 0))
    return pl.pallas_call(
        _ffn_ln_kernel,
        grid=(n_tok // tm,),
        in_specs=[tile(), _const_spec(w_i.shape), _const_spec(w_o.shape),
                  _const_spec((1, d_model)), _const_spec((1, d_model))],
        out_specs=tile(),
        out_shape=jax.ShapeDtypeStruct((n_tok, d_model), F32),
        scratch_shapes=[pltpu.VMEM((tm, d_ff), BF16)],
        compiler_params=_params(1),
        name="ffn_ln",
    )(h2, w_i, w_o, ln_g, ln_b)


def _tile_sizes(seq):
    ts = 64 if seq % 64 == 0 else seq
    tm = 512 if seq % 512 == 0 else seq
    return ts, tm


def kernel(x, mem, w_in, conv_w, conv_b, w_rg, b_rg, w_ig, b_ig, lru_lambda, w_br_rnn, w_br_attn, sinks, w_out, ln1_g, ln1_b, cq_w, ckv_w, co_w, ln2_g, ln2_b, ffn_wi, ffn_wo, ln3_g, ln3_b):
    n_batch, seq, d_model = x.shape
    depth = w_in.shape[0]
    n_tok = n_batch * seq
    kv_width = N_KV_HEADS * HEAD_DIM
    n_a = 3 * d_model + 2 * kv_width
    ts, tm = _tile_sizes(seq)
    rope_tabs = _rope_tables(seq)
    row = lambda p: p.reshape(1, -1)

    h = x
    for l in range(depth):
        w_a = w_in[l, :, :n_a].astype(BF16)
        w_g = w_in[l, :, n_a:].astype(BF16)
        w_gate = jnp.concatenate([w_rg[l], w_ig[l]], axis=-1).astype(BF16)
        y_rnn, q, kv = _rglru_qkv(h, w_a, w_gate, conv_w[l], row(conv_b[l]), row(b_rg[l]),
                                    row(b_ig[l]), row(lru_lambda[l]), rope_tabs, ts=ts)
        y_attn = _swa(q, kv, sinks[l])
        h2 = _merge_ln(h.reshape(n_tok, d_model), y_rnn.reshape(n_tok, d_model),
                       y_attn.reshape(n_tok, d_model), w_g, w_br_rnn[l].astype(BF16),
                       w_br_attn[l].astype(BF16), w_out[l].astype(BF16),
                       row(ln1_g[l]), row(ln1_b[l]), tm=tm)
        h = _cross_ln(h2.reshape(n_batch, seq, d_model), mem, cq_w[l].astype(BF16),
                      ckv_w[l].astype(BF16), co_w[l].astype(BF16), row(ln2_g[l]), row(ln2_b[l]),
                      tm=tm)
        h2 = _ffn_ln(h.reshape(n_tok, d_model), ffn_wi[l].astype(BF16), ffn_wo[l].astype(BF16),
                     row(ln3_g[l]), row(ln3_b[l]), tm=tm)
        h = h2.reshape(n_batch, seq, d_model)
    return h
```

```python
import functools
import math

import jax
import jax.numpy as jnp
from jax import lax
from jax.experimental import pallas as pl
from jax.experimental.pallas import tpu as pltpu

F32 = jnp.float32
BF16 = jnp.bfloat16

DEPTH = 4
RNN_BLOCKS = 4
CONV_WIDTH = 4
LRU_C = 8.0
HEAD_DIM = 64
N_KV_HEADS = 2
WINDOW = 128
ROPE_THETA = 500000.0
ROT_DIM = HEAD_DIM // 4
CROSS_HEADS = 4
LN_EPS = 1e-5
DEEPNORM_ALPHA = (2 * DEPTH) ** 0.25
NEG_INF = -1e30

SUBLANES = 8
LANES = 128
MXU_DIM = 256
VMEM_LIMIT_BYTES = 56 * 1024 * 1024


def _params(n_grid_dims):
    return pltpu.CompilerParams(
        dimension_semantics=("arbitrary",) * n_grid_dims,
        vmem_limit_bytes=VMEM_LIMIT_BYTES,
    )


def _const_spec(shape):
    zeros = (0,) * len(shape)
    return pl.BlockSpec(shape, lambda *_: zeros, pipeline_mode=pl.Buffered(1))


def _layer_norm(x, g, b):
    mu = jnp.mean(x, axis=-1, keepdims=True)
    xc = x - mu
    var = jnp.mean(xc * xc, axis=-1, keepdims=True)
    return xc * lax.rsqrt(var + LN_EPS) * g + b


def _sigmoid(x):
    return 0.5 * jnp.tanh(0.5 * x) + 0.5


def _sqrt_nonneg(x):
    return jnp.where(x > 0.0, x * lax.rsqrt(x), 0.0)


def _dot(a, b):
    return jnp.dot(a, b, preferred_element_type=F32)


def _dot_nt(a, b):
    return lax.dot_general(a, b, (((1,), (1,)), ((), ())), preferred_element_type=F32)


def _rglru_qkv_kernel(x_ref, w_ref, wgate_ref, convw_ref, convb_ref, brg_ref, big_ref, lam_ref,
                      rc_ref, rs1_ref, rs2_ref,
                      y_ref, q_ref, kv_ref,
                      tm_s, stage_s, ab_s, convc_s, hc_s, *, n_batch, ts, d_model, kv_width):
    rows = n_batch * ts
    rnn_block = d_model // RNN_BLOCKS
    tail = (CONV_WIDTH - 1) * n_batch
    pitch = _batch_pitch(ts)

    @pl.when(pl.program_id(0) == 0)
    def _():
        convc_s[...] = jnp.zeros_like(convc_s)
        hc_s[...] = jnp.zeros_like(hc_s)

    xs = jnp.concatenate([x_ref[b] for b in range(n_batch)], axis=0).astype(BF16)

    def to_time_major(val, slot):
        n_blk = val.shape[1] // LANES
        for j in range(n_blk):
            for b in range(n_batch):
                tm_s[slot * n_blk + j, b * pitch:b * pitch + ts, :] = (
                    val[b * ts:(b + 1) * ts, j * LANES:(j + 1) * LANES])
        return jnp.concatenate(
            [jnp.concatenate([tm_s[slot * n_blk + j, pl.ds(t, n_batch, stride=pitch), :]
                              for j in range(n_blk)], axis=1)
             for t in range(ts)], axis=0)

    rc, rs1, rs2 = rc_ref[...], rs1_ref[...], rs2_ref[...]
    half = ROT_DIM // 2

    def rope(tile, tabs):
        c, s1, s2 = tabs
        return (tile * c + pltpu.roll(tile, LANES - half, axis=1) * s1
                + pltpu.roll(tile, half, axis=1) * s2)

    q_off = 2 * d_model
    scale = HEAD_DIM ** -0.5
    k_tabs = (rc, rs1, rs2)
    q_tabs = (rc * scale, rs1 * scale, rs2 * scale)

    def emit_q(c):
        qc = _dot(xs, w_ref[:, q_off + c * MXU_DIM:q_off + (c + 1) * MXU_DIM])
        for b in range(n_batch):
            qb = qc[b * ts:(b + 1) * ts, :]
            parts = [rope(qb[:, j * LANES:(j + 1) * LANES], q_tabs) for j in range(MXU_DIM // LANES)]
            q_ref[b, :, c * MXU_DIM:(c + 1) * MXU_DIM] = jnp.concatenate(parts, axis=1).astype(BF16)

    def emit_kv():
        kv_off = 3 * d_model
        kv = _dot(xs, w_ref[:, kv_off:kv_off + 2 * kv_width])
        for b in range(n_batch):
            kvb = kv[b * ts:(b + 1) * ts, :]
            kb = rope(kvb[:, :kv_width], k_tabs)
            vb = kvb[:, kv_width:]
            kv_ref[b] = jnp.concatenate(
                [kb, pltpu.roll(kb, HEAD_DIM, axis=1), vb, pltpu.roll(vb, HEAD_DIM, axis=1)],
                axis=1).astype(BF16)

    neg_c_softplus = -LRU_C * jax.nn.softplus(-lam_ref[...])

    for n in range(RNN_BLOCKS):
        cs = slice(n * rnn_block, (n + 1) * rnn_block)
        xr = to_time_major(_dot(xs, w_ref[:, cs]), n)
        hist = convc_s[:, cs]
        ext = jnp.concatenate([hist, xr], axis=0)
        convc_s[:, cs] = xr[rows - tail:, :]
        xc = convb_ref[:, cs]
        for kk in range(CONV_WIDTH):
            xc = xc + ext[kk * n_batch:kk * n_batch + rows, :] * convw_ref[kk:kk + 1, cs]
        gates = _dot(xc.astype(BF16), wgate_ref[n])
        emit_q(n)
        r = _sigmoid(gates[:, :rnn_block] + brg_ref[:, cs])
        i = _sigmoid(gates[:, rnn_block:] + big_ref[:, cs])
        log_a = neg_c_softplus[:, cs] * r
        a = jnp.exp(log_a)
        mult = _sqrt_nonneg(-jnp.tanh(log_a) * (a * a + 1.0))
        ab_s[0, :, cs] = a
        ab_s[1, :, cs] = mult * (i * xc)

    h = hc_s[...]
    for t in range(ts):
        rs = slice(t * n_batch, (t + 1) * n_batch)
        h = ab_s[0, rs, :] * h + ab_s[1, rs, :]
        for j in range(d_model // LANES):
            stage_s[j, rs, :] = h[:, j * LANES:(j + 1) * LANES]
    hc_s[...] = h

    blk_lanes = rnn_block // LANES
    for n in range(RNN_BLOCKS):
        gate = jax.nn.gelu(
            _dot(xs, w_ref[:, d_model + n * rnn_block:d_model + (n + 1) * rnn_block]))
        for b in range(n_batch):
            hb = jnp.concatenate(
                [stage_s[n * blk_lanes + j, pl.ds(b, ts, stride=n_batch), :] for j in range(blk_lanes)],
                axis=1)
            y_ref[b, :, n * rnn_block:(n + 1) * rnn_block] = (
                hb * gate[b * ts:(b + 1) * ts, :]).astype(BF16)
    emit_kv()


def _batch_pitch(ts):
    groups = ts // SUBLANES
    return SUBLANES * (groups + 1 if groups % 2 == 0 else groups)


def _rope_tables(seq_len):
    half = ROT_DIM // 2
    pos = jnp.arange(seq_len, dtype=F32)
    inv_freq = ROPE_THETA ** (-jnp.arange(0, ROT_DIM, 2, dtype=F32) / ROT_DIM)
    ang = pos[:, None] * inv_freq[None, :]
    cos, sin = jnp.cos(ang), jnp.sin(ang)
    ones = jnp.ones((seq_len, HEAD_DIM - ROT_DIM), F32)
    zeros = jnp.zeros((seq_len, HEAD_DIM - ROT_DIM), F32)
    zh = jnp.zeros((seq_len, half), F32)
    c_head = jnp.concatenate([cos, cos, ones], axis=1)
    s1_head = jnp.concatenate([-sin, zh, zeros], axis=1)
    s2_head = jnp.concatenate([zh, sin, zeros], axis=1)
    reps = LANES // HEAD_DIM
    return (jnp.tile(c_head, (1, reps)), jnp.tile(s1_head, (1, reps)), jnp.tile(s2_head, (1, reps)))


def _rglru_qkv(h, w_a, w_gate, conv_w, conv_b, b_rg, b_ig, lam, rope_tabs, *, ts):
    n_batch, seq, d_model = h.shape
    kv_width = N_KV_HEADS * HEAD_DIM
    assert n_batch == SUBLANES and seq % ts == 0 and ts % SUBLANES == 0 and kv_width == LANES
    rows = n_batch * ts
    n_cols = w_a.shape[1]
    rnn_block = d_model // RNN_BLOCKS
    kernel = functools.partial(_rglru_qkv_kernel, n_batch=n_batch, ts=ts, d_model=d_model,
                               kv_width=kv_width)
    tile3 = lambda width: pl.BlockSpec((n_batch, ts, width), lambda i: (0, i, 0))
    rope_spec = pl.BlockSpec((ts, LANES), lambda i: (i, 0))
    return pl.pallas_call(
        kernel,
        grid=(seq // ts,),
        in_specs=[
            tile3(d_model),
            _const_spec((d_model, n_cols)),
            _const_spec((RNN_BLOCKS, rnn_block, 2 * rnn_block)),
            _const_spec((CONV_WIDTH, d_model)),
            _const_spec((1, d_model)), _const_spec((1, d_model)), _const_spec((1, d_model)),
            _const_spec((1, d_model)),
            rope_spec, rope_spec, rope_spec,
        ],
        out_specs=[tile3(d_model), tile3(d_model), tile3(4 * kv_width)],
        out_shape=[
            jax.ShapeDtypeStruct((n_batch, seq, d_model), BF16),
            jax.ShapeDtypeStruct((n_batch, seq, d_model), BF16),
            jax.ShapeDtypeStruct((n_batch, seq, 4 * kv_width), BF16),
        ],
        scratch_shapes=[
            pltpu.VMEM((d_model // LANES, n_batch * _batch_pitch(ts), LANES), F32),
            pltpu.VMEM((d_model // LANES, rows, LANES), F32),
            pltpu.VMEM((2, rows, d_model), F32),
            pltpu.VMEM(((CONV_WIDTH - 1) * n_batch, d_model), F32),
            pltpu.VMEM((n_batch, d_model), F32),
        ],
        compiler_params=_params(1),
        name="rglru_qkv",
    )(h, w_a, w_gate, conv_w, conv_b, b_rg, b_ig, lam, *rope_tabs)


def _swa_kernel(sinks_ref, q_ref, kvp_ref, kvc_ref, o_ref, *, n_q_heads, blk):
    n_sub = q_ref.shape[0] // blk
    group = n_q_heads // N_KV_HEADS
    pairs = group // 2
    row = lax.broadcasted_iota(jnp.int32, (blk, blk), 0)
    col = lax.broadcasted_iota(jnp.int32, (blk, blk), 1)
    upper = col > row
    no_prev = jnp.where(pl.program_id(1) == 0, NEG_INF, 0.0)
    first_bias = jnp.where(upper, no_prev, 0.0)
    low_lanes = lax.broadcasted_iota(jnp.int32, (blk, LANES), 1) < HEAD_DIM
    low2 = lax.broadcasted_iota(jnp.int32, (2 * blk, LANES), 1) < HEAD_DIM
    head_ind = jnp.concatenate([jnp.where(low2, 1.0, 0.0), jnp.where(low2, 0.0, 1.0)],
                               axis=0).astype(BF16)

    def keep(x, low):
        return jnp.where(low_lanes if low else ~low_lanes, x, jnp.zeros_like(x))

    for sb in range(n_sub):
        rows = slice(sb * blk, (sb + 1) * blk)

        def prev(j):
            if sb == 0:
                return kvp_ref[:, j * LANES:(j + 1) * LANES]
            return kvc_ref[(sb - 1) * blk:sb * blk, j * LANES:(j + 1) * LANES]

        def cur(j):
            return kvc_ref[rows, j * LANES:(j + 1) * LANES]

        for g in range(N_KV_HEADS):
            low_src, high_src = (0, 1) if g == 0 else (1, 0)
            kk = jnp.concatenate([keep(prev(low_src), True), keep(cur(low_src), True),
                                  keep(prev(high_src), False), keep(cur(high_src), False)], axis=0)
            vv = jnp.concatenate([keep(prev(2 + low_src), True), keep(cur(2 + low_src), True),
                                  keep(prev(2 + high_src), False), keep(cur(2 + high_src), False)], axis=0)
            qg = jnp.concatenate(
                [q_ref[rows, (g * pairs + p) * LANES:(g * pairs + p + 1) * LANES] for p in range(pairs)], axis=0)
            s = _dot_nt(qg, kk)
            probs, sink_logits = [], []
            for p in range(pairs):
                sp = s[p * blk:(p + 1) * blk, :]
                row_probs, rel = [], []
                for half in range(2):
                    sink = sinks_ref[g * group + 2 * p + half]
                    comb = jnp.where(upper, sp[:, 2 * half * blk:(2 * half + 1) * blk],
                                     sp[:, (2 * half + 1) * blk:(2 * half + 2) * blk])
                    if sb == 0:
                        comb = comb + first_bias
                    m = jnp.maximum(jnp.max(comb, axis=-1, keepdims=True), sink)
                    e = jnp.exp(comb - m)
                    rel.append(sink - m)
                    row_probs += [jnp.where(upper, e, 0.0).astype(BF16), jnp.where(upper, 0.0, e).astype(BF16)]
                probs.append(jnp.concatenate(row_probs, axis=1))
                sink_logits.append(jnp.where(low_lanes, rel[0], rel[1]))
            o = _dot(jnp.concatenate(probs, axis=0), jnp.concatenate([vv, head_ind], axis=1))
            for p in range(pairs):
                prs = slice(p * blk, (p + 1) * blk)
                denom = o[prs, LANES:] + jnp.exp(sink_logits[p])
                j = g * pairs + p
                o_ref[rows, j * LANES:(j + 1) * LANES] = (o[prs, :LANES] * (1.0 / denom)).astype(o_ref.dtype)


def _swa(q, kv, sinks, *, n_sub):
    n_batch, seq, width = q.shape
    blk = WINDOW
    n_q_heads = width // HEAD_DIM
    assert 2 * HEAD_DIM == LANES and N_KV_HEADS == 2 and kv.shape[-1] == 4 * LANES
    assert seq % (n_sub * blk) == 0
    prev = lambda b, n: (b, jnp.maximum(n_sub * n - 1, 0), 0)
    cur = lambda b, n: (b, n, 0)
    return pl.pallas_call(
        functools.partial(_swa_kernel, n_q_heads=n_q_heads, blk=blk),
        grid=(n_batch, seq // (n_sub * blk)),
        in_specs=[
            pl.BlockSpec(memory_space=pltpu.SMEM),
            pl.BlockSpec((None, n_sub * blk, width), cur),
            pl.BlockSpec((None, blk, kv.shape[-1]), prev),
            pl.BlockSpec((None, n_sub * blk, kv.shape[-1]), cur),
        ],
        out_specs=pl.BlockSpec((None, n_sub * blk, width), cur),
        out_shape=jax.ShapeDtypeStruct((n_batch, seq, width), BF16),
        compiler_params=_params(2),
        name="swa",
    )(sinks, q, kv, kv)


def _merge_ln_kernel(h_ref, yr_ref, ya_ref, wg_ref, wbr_ref, wba_ref, wo_ref, g_ref, b_ref, o_ref,
                     merged_s):
    d_model = h_ref.shape[1]
    h = h_ref[...]
    hb = h.astype(BF16)
    yr = yr_ref[...]
    ya = ya_ref[...]
    for c in range(d_model // MXU_DIM):
        cs = slice(c * MXU_DIM, (c + 1) * MXU_DIM)
        cs2 = slice(d_model + c * MXU_DIM, d_model + (c + 1) * MXU_DIM)
        gate_r = _sigmoid(_dot(hb, wg_ref[:, cs]))
        gate_a = _sigmoid(_dot(hb, wg_ref[:, cs2]))
        merged = gate_r * _dot(yr, wbr_ref[:, cs]) + gate_a * _dot(ya, wba_ref[:, cs])
        merged_s[:, cs] = merged.astype(BF16)
    mix = _dot(merged_s[...], wo_ref[...])
    o_ref[...] = _layer_norm(DEEPNORM_ALPHA * h + mix, g_ref[...], b_ref[...])


def _merge_ln(h2, yr2, ya2, w_g, w_br, w_ba, w_o, ln_g, ln_b, *, tm):
    n_tok, d_model = h2.shape
    tile = lambda: pl.BlockSpec((tm, d_model), lambda i: (i, 0))
    return pl.pallas_call(
        _merge_ln_kernel,
        grid=(n_tok // tm,),
        in_specs=[tile(), tile(), tile(),
                  _const_spec(w_g.shape), _const_spec(w_br.shape), _const_spec(w_ba.shape),
                  _const_spec(w_o.shape), _const_spec((1, d_model)), _const_spec((1, d_model))],
        out_specs=tile(),
        out_shape=jax.ShapeDtypeStruct((n_tok, d_model), F32),
        scratch_shapes=[pltpu.VMEM((tm, d_model), BF16)],
        compiler_params=_params(1),
        name="merge_ln",
    )(h2, yr2, ya2, w_g, w_br, w_ba, w_o, ln_g, ln_b)


def _cross_ln_kernel(h_ref, mem_ref, wq_ref, wkv_ref, wo_ref, g_ref, b_ref, o_ref, qk_s, vo_s, p_s):
    d_model = h_ref.shape[1]
    mem_len = mem_ref.shape[0]
    head_dim = d_model // CROSS_HEADS
    scale = head_dim ** -0.5

    @pl.when(pl.program_id(1) == 0)
    def _():
        kv = _dot(mem_ref[...].astype(BF16), wkv_ref[...])
        for hd in range(CROSS_HEADS):
            cs = slice(hd * head_dim, (hd + 1) * head_dim)
            ms = slice(hd * mem_len, (hd + 1) * mem_len)
            k = (kv[:, cs] * scale).astype(BF16)
            v = kv[:, d_model + hd * head_dim:d_model + (hd + 1) * head_dim].astype(BF16)
            qk_s[:, ms] = _dot_nt(wq_ref[:, cs], k).astype(BF16)
            vo_s[ms, :] = _dot(v, wo_ref[cs, :]).astype(BF16)

    h = h_ref[...]
    s = _dot(h.astype(BF16), qk_s[...])
    for hd in range(CROSS_HEADS):
        ms = slice(hd * mem_len, (hd + 1) * mem_len)
        sh = s[:, ms]
        e = jnp.exp(sh - jnp.max(sh, axis=-1, keepdims=True))
        p_s[:, ms] = (e * (1.0 / jnp.sum(e, axis=-1, keepdims=True))).astype(BF16)
    out = _dot(p_s[...], vo_s[...])
    o_ref[...] = _layer_norm(DEEPNORM_ALPHA * h + out, g_ref[...], b_ref[...])


def _cross_ln(h, mem, w_q, w_kv, w_o, ln_g, ln_b, *, tm):
    n_batch, seq, d_model = h.shape
    mem_len = mem.shape[1]
    tile = pl.BlockSpec((None, tm, d_model), lambda b, i: (b, i, 0))
    return pl.pallas_call(
        _cross_ln_kernel,
        grid=(n_batch, seq // tm),
        in_specs=[tile,
                  pl.BlockSpec((None, mem_len, d_model), lambda b, i: (b, 0, 0)),
                  _const_spec(w_q.shape), _const_spec(w_kv.shape), _const_spec(w_o.shape),
                  _const_spec((1, d_model)), _const_spec((1, d_model))],
        out_specs=tile,
        out_shape=jax.ShapeDtypeStruct((n_batch, seq, d_model), F32),
        scratch_shapes=[pltpu.VMEM((d_model, CROSS_HEADS * mem_len), BF16),
                        pltpu.VMEM((CROSS_HEADS * mem_len, d_model), BF16),
                        pltpu.VMEM((tm, CROSS_HEADS * mem_len), BF16)],
        compiler_params=_params(2),
        name="cross_ln",
    )(h, mem, w_q, w_kv, w_o, ln_g, ln_b)


def _ffn_ln_kernel(h_ref, wi_ref, wo_ref, g_ref, b_ref, o_ref, act_s):
    d_ff = wo_ref.shape[0]
    h = h_ref[...]
    hb = h.astype(BF16)
    for c in range(d_ff // MXU_DIM):
        cs = slice(c * MXU_DIM, (c + 1) * MXU_DIM)
        cs_up = slice(d_ff + c * MXU_DIM, d_ff + (c + 1) * MXU_DIM)
        gate = _dot(hb, wi_ref[:, cs])
        up = _dot(hb, wi_ref[:, cs_up])
        act_s[:, cs] = (jax.nn.silu(gate) * up).astype(BF16)
    out = _dot(act_s[...], wo_ref[...])
    o_ref[...] = _layer_norm(DEEPNORM_ALPHA * h + out, g_ref[...], b_ref[...])


def _ffn_ln(h2, w_i, w_o, ln_g, ln_b, *, tm):
    n_tok, d_model = h2.shape
    d_ff = w_o.shape[0]
    assert d_ff % MXU_DIM == 0
    tile = lambda: pl.BlockSpec((tm, d_model), lambda i: (i, 0))
    return pl.pallas_call(
        _ffn_ln_kernel,
        grid=(n_tok // tm,),
        in_specs=[tile(), _const_spec(w_i.shape), _const_spec(w_o.shape),
                  _const_spec((1, d_model)), _const_spec((1, d_model))],
        out_specs=tile(),
        out_shape=jax.ShapeDtypeStruct((n_tok, d_model), F32),
        scratch_shapes=[pltpu.VMEM((tm, d_ff), BF16)],
        compiler_params=_params(1),
        name="ffn_ln",
    )(h2, w_i, w_o, ln_g, ln_b)


def _tile_sizes(seq):
    ts = 128 if seq % 128 == 0 else seq
    tm = 1024 if seq % 1024 == 0 else seq
    return ts, tm


def kernel(x, mem, w_in, conv_w, conv_b, w_rg, b_rg, w_ig, b_ig, lru_lambda, w_br_rnn, w_br_attn, sinks, w_out, ln1_g, ln1_b, cq_w, ckv_w, co_w, ln2_g, ln2_b, ffn_wi, ffn_wo, ln3_g, ln3_b):
    n_batch, seq, d_model = x.shape
    depth = w_in.shape[0]
    n_tok = n_batch * seq
    kv_width = N_KV_HEADS * HEAD_DIM
    n_a = 3 * d_model + 2 * kv_width
    ts, tm = _tile_sizes(seq)
    rope_tabs = _rope_tables(seq)
    row = lambda p: p.reshape(1, -1)

    h = x
    for l in range(depth):
        w_a = w_in[l, :, :n_a].astype(BF16)
        w_g = w_in[l, :, n_a:].astype(BF16)
        w_gate = jnp.concatenate([w_rg[l], w_ig[l]], axis=-1).astype(BF16)
        y_rnn, q, kv = _rglru_qkv(h, w_a, w_gate, conv_w[l], row(conv_b[l]), row(b_rg[l]),
                                    row(b_ig[l]), row(lru_lambda[l]), rope_tabs, ts=ts)
        y_attn = _swa(q, kv, sinks[l], n_sub=8 if seq % (8 * WINDOW) == 0 else 1)
        h2 = _merge_ln(h.reshape(n_tok, d_model), y_rnn.reshape(n_tok, d_model),
                       y_attn.reshape(n_tok, d_model), w_g, w_br_rnn[l].astype(BF16),
                       w_br_attn[l].astype(BF16), w_out[l].astype(BF16),
                       row(ln1_g[l]), row(ln1_b[l]), tm=tm)
        h = _cross_ln(h2.reshape(n_batch, seq, d_model), mem, cq_w[l].astype(BF16),
                      ckv_w[l].astype(BF16), co_w[l].astype(BF16), row(ln2_g[l]), row(ln2_b[l]),
                      tm=tm)
        h2 = _ffn_ln(h.reshape(n_tok, d_model), ffn_wi[l].astype(BF16), ffn_wo[l].astype(BF16),
                     row(ln3_g[l]), row(ln3_b[l]), tm=tm)
        h = h2.reshape(n_batch, seq, d_model)
    return h
```

```python
import functools
import math

import jax
import jax.numpy as jnp
from jax import lax
from jax.experimental import pallas as pl
from jax.experimental.pallas import tpu as pltpu

F32 = jnp.float32
BF16 = jnp.bfloat16

DEPTH = 4
RNN_BLOCKS = 4
CONV_WIDTH = 4
LRU_C = 8.0
HEAD_DIM = 64
N_KV_HEADS = 2
WINDOW = 128
ROPE_THETA = 500000.0
ROT_DIM = HEAD_DIM // 4
CROSS_HEADS = 4
LN_EPS = 1e-5
DEEPNORM_ALPHA = (2 * DEPTH) ** 0.25
NEG_INF = -1e30

SUBLANES = 8
LANES = 128
MXU_DIM = 256
VMEM_LIMIT_BYTES = 56 * 1024 * 1024


def _params(n_grid_dims):
    return pltpu.CompilerParams(
        dimension_semantics=("arbitrary",) * n_grid_dims,
        vmem_limit_bytes=VMEM_LIMIT_BYTES,
    )


def _layer_spec(stack, layer):
    zeros = (0,) * (stack.ndim - 1)
    return pl.BlockSpec((None,) + stack.shape[1:], lambda *_: (layer,) + zeros,
                        pipeline_mode=pl.Buffered(1))


def _layer_norm(x, g, b):
    mu = jnp.mean(x, axis=-1, keepdims=True)
    xc = x - mu
    var = jnp.mean(xc * xc, axis=-1, keepdims=True)
    return xc * lax.rsqrt(var + LN_EPS) * g + b


def _sigmoid(x):
    return 0.5 * jnp.tanh(0.5 * x) + 0.5


def _sqrt_nonneg(x):
    return jnp.where(x > 0.0, x * lax.rsqrt(x), 0.0)


def _dot(a, b):
    return jnp.dot(a, b, preferred_element_type=F32)


def _dot_nt(a, b):
    return lax.dot_general(a, b, (((1,), (1,)), ((), ())), preferred_element_type=F32)


def _rglru_qkv_kernel(x_ref, w_ref, wgate_ref, convw_ref, convb_ref, brg_ref, big_ref, lam_ref,
                      rc_ref, rs1_ref, rs2_ref,
                      y_ref, q_ref, kv_ref,
                      tm_s, stage_s, ab_s, convc_s, hc_s, *, n_batch, ts, d_model, kv_width):
    rows = n_batch * ts
    rnn_block = d_model // RNN_BLOCKS
    tail = (CONV_WIDTH - 1) * n_batch
    pitch = _batch_pitch(ts)

    @pl.when(pl.program_id(0) == 0)
    def _():
        convc_s[...] = jnp.zeros_like(convc_s)
        hc_s[...] = jnp.zeros_like(hc_s)

    xs = jnp.concatenate([x_ref[b] for b in range(n_batch)], axis=0).astype(BF16)

    def to_time_major(val, slot):
        n_blk = val.shape[1] // LANES
        for j in range(n_blk):
            for b in range(n_batch):
                tm_s[slot * n_blk + j, b * pitch:b * pitch + ts, :] = (
                    val[b * ts:(b + 1) * ts, j * LANES:(j + 1) * LANES])
        return jnp.concatenate(
            [jnp.concatenate([tm_s[slot * n_blk + j, pl.ds(t, n_batch, stride=pitch), :]
                              for j in range(n_blk)], axis=1)
             for t in range(ts)], axis=0)

    rc, rs1, rs2 = rc_ref[...], rs1_ref[...], rs2_ref[...]
    half = ROT_DIM // 2

    def rope(tile, tabs):
        c, s1, s2 = tabs
        return (tile * c + pltpu.roll(tile, LANES - half, axis=1) * s1
                + pltpu.roll(tile, half, axis=1) * s2)

    q_off = 2 * d_model
    scale = HEAD_DIM ** -0.5
    k_tabs = (rc, rs1, rs2)
    q_tabs = (rc * scale, rs1 * scale, rs2 * scale)

    def emit_q(c):
        qc = _dot(xs, w_ref[:, q_off + c * MXU_DIM:q_off + (c + 1) * MXU_DIM])
        for b in range(n_batch):
            qb = qc[b * ts:(b + 1) * ts, :]
            parts = [rope(qb[:, j * LANES:(j + 1) * LANES], q_tabs) for j in range(MXU_DIM // LANES)]
            q_ref[b, :, c * MXU_DIM:(c + 1) * MXU_DIM] = jnp.concatenate(parts, axis=1).astype(BF16)

    def emit_kv():
        kv_off = 3 * d_model
        kv = _dot(xs, w_ref[:, kv_off:kv_off + 2 * kv_width])
        for b in range(n_batch):
            kvb = kv[b * ts:(b + 1) * ts, :]
            kb = rope(kvb[:, :kv_width], k_tabs)
            vb = kvb[:, kv_width:]
            kv_ref[b] = jnp.concatenate(
                [kb, pltpu.roll(kb, HEAD_DIM, axis=1), vb, pltpu.roll(vb, HEAD_DIM, axis=1)],
                axis=1).astype(BF16)

    neg_c_softplus = -LRU_C * jax.nn.softplus(-lam_ref[...])

    for n in range(RNN_BLOCKS):
        cs = slice(n * rnn_block, (n + 1) * rnn_block)
        xr = to_time_major(_dot(xs, w_ref[:, cs]), n)
        hist = convc_s[:, cs]
        ext = jnp.concatenate([hist, xr], axis=0)
        convc_s[:, cs] = xr[rows - tail:, :]
        xc = convb_ref[:, cs]
        for kk in range(CONV_WIDTH):
            xc = xc + ext[kk * n_batch:kk * n_batch + rows, :] * convw_ref[kk:kk + 1, cs]
        gates = _dot(xc.astype(BF16), wgate_ref[n])
        emit_q(n)
        r = _sigmoid(gates[:, :rnn_block] + brg_ref[:, cs])
        i = _sigmoid(gates[:, rnn_block:] + big_ref[:, cs])
        log_a = neg_c_softplus[:, cs] * r
        a = jnp.exp(log_a)
        mult = _sqrt_nonneg(-jnp.tanh(log_a) * (a * a + 1.0))
        ab_s[0, :, cs] = a
        ab_s[1, :, cs] = mult * (i * xc)

    h = hc_s[...]
    for t in range(ts):
        rs = slice(t * n_batch, (t + 1) * n_batch)
        h = ab_s[0, rs, :] * h + ab_s[1, rs, :]
        for j in range(d_model // LANES):
            stage_s[j, rs, :] = h[:, j * LANES:(j + 1) * LANES]
    hc_s[...] = h

    blk_lanes = rnn_block // LANES
    for n in range(RNN_BLOCKS):
        gate = jax.nn.gelu(
            _dot(xs, w_ref[:, d_model + n * rnn_block:d_model + (n + 1) * rnn_block]))
        for b in range(n_batch):
            hb = jnp.concatenate(
                [stage_s[n * blk_lanes + j, pl.ds(b, ts, stride=n_batch), :] for j in range(blk_lanes)],
                axis=1)
            y_ref[b, :, n * rnn_block:(n + 1) * rnn_block] = (
                hb * gate[b * ts:(b + 1) * ts, :]).astype(BF16)
    emit_kv()


def _batch_pitch(ts):
    groups = ts // SUBLANES
    return SUBLANES * (groups + 1 if groups % 2 == 0 else groups)


def _rope_tables(seq_len):
    half = ROT_DIM // 2
    pos = jnp.arange(seq_len, dtype=F32)
    inv_freq = ROPE_THETA ** (-jnp.arange(0, ROT_DIM, 2, dtype=F32) / ROT_DIM)
    ang = pos[:, None] * inv_freq[None, :]
    cos, sin = jnp.cos(ang), jnp.sin(ang)
    ones = jnp.ones((seq_len, HEAD_DIM - ROT_DIM), F32)
    zeros = jnp.zeros((seq_len, HEAD_DIM - ROT_DIM), F32)
    zh = jnp.zeros((seq_len, half), F32)
    c_head = jnp.concatenate([cos, cos, ones], axis=1)
    s1_head = jnp.concatenate([-sin, zh, zeros], axis=1)
    s2_head = jnp.concatenate([zh, sin, zeros], axis=1)
    reps = LANES // HEAD_DIM
    return (jnp.tile(c_head, (1, reps)), jnp.tile(s1_head, (1, reps)), jnp.tile(s2_head, (1, reps)))


def _rglru_qkv(h, w_a, w_gate, conv_w, conv_b, b_rg, b_ig, lam, rope_tabs, *, layer, ts):
    n_batch, seq, d_model = h.shape
    kv_width = N_KV_HEADS * HEAD_DIM
    assert n_batch == SUBLANES and seq % ts == 0 and ts % SUBLANES == 0 and kv_width == LANES
    rows = n_batch * ts
    kernel = functools.partial(_rglru_qkv_kernel, n_batch=n_batch, ts=ts, d_model=d_model,
                               kv_width=kv_width)
    tile3 = lambda width: pl.BlockSpec((n_batch, ts, width), lambda i: (0, i, 0))
    rope_spec = pl.BlockSpec((ts, LANES), lambda i: (i, 0))
    return pl.pallas_call(
        kernel,
        grid=(seq // ts,),
        in_specs=[
            tile3(d_model),
            *[_layer_spec(p, layer) for p in (w_a, w_gate, conv_w, conv_b, b_rg, b_ig, lam)],
            rope_spec, rope_spec, rope_spec,
        ],
        out_specs=[tile3(d_model), tile3(d_model), tile3(4 * kv_width)],
        out_shape=[
            jax.ShapeDtypeStruct((n_batch, seq, d_model), BF16),
            jax.ShapeDtypeStruct((n_batch, seq, d_model), BF16),
            jax.ShapeDtypeStruct((n_batch, seq, 4 * kv_width), BF16),
        ],
        scratch_shapes=[
            pltpu.VMEM((d_model // LANES, n_batch * _batch_pitch(ts), LANES), F32),
            pltpu.VMEM((d_model // LANES, rows, LANES), F32),
            pltpu.VMEM((2, rows, d_model), F32),
            pltpu.VMEM(((CONV_WIDTH - 1) * n_batch, d_model), F32),
            pltpu.VMEM((n_batch, d_model), F32),
        ],
        compiler_params=_params(1),
        name="rglru_qkv",
    )(h, w_a, w_gate, conv_w, conv_b, b_rg, b_ig, lam, *rope_tabs)


def _swa_kernel(sinks_ref, q_ref, kvp_ref, kvc_ref, o_ref, *, layer, n_q_heads, blk):
    n_sub = q_ref.shape[0] // blk
    group = n_q_heads // N_KV_HEADS
    pairs = group // 2
    row = lax.broadcasted_iota(jnp.int32, (blk, blk), 0)
    col = lax.broadcasted_iota(jnp.int32, (blk, blk), 1)
    upper = col > row
    no_prev = jnp.where(pl.program_id(1) == 0, NEG_INF, 0.0)
    first_bias = jnp.where(upper, no_prev, 0.0)
    low_lanes = lax.broadcasted_iota(jnp.int32, (blk, LANES), 1) < HEAD_DIM
    low2 = lax.broadcasted_iota(jnp.int32, (2 * blk, LANES), 1) < HEAD_DIM
    head_ind = jnp.concatenate([jnp.where(low2, 1.0, 0.0), jnp.where(low2, 0.0, 1.0)],
                               axis=0).astype(BF16)

    def keep(x, low):
        return jnp.where(low_lanes if low else ~low_lanes, x, jnp.zeros_like(x))

    for sb in range(n_sub):
        rows = slice(sb * blk, (sb + 1) * blk)

        def prev(j):
            if sb == 0:
                return kvp_ref[:, j * LANES:(j + 1) * LANES]
            return kvc_ref[(sb - 1) * blk:sb * blk, j * LANES:(j + 1) * LANES]

        def cur(j):
            return kvc_ref[rows, j * LANES:(j + 1) * LANES]

        for g in range(N_KV_HEADS):
            low_src, high_src = (0, 1) if g == 0 else (1, 0)
            kk = jnp.concatenate([keep(prev(low_src), True), keep(cur(low_src), True),
                                  keep(prev(high_src), False), keep(cur(high_src), False)], axis=0)
            vv = jnp.concatenate([keep(prev(2 + low_src), True), keep(cur(2 + low_src), True),
                                  keep(prev(2 + high_src), False), keep(cur(2 + high_src), False)], axis=0)
            qg = jnp.concatenate(
                [q_ref[rows, (g * pairs + p) * LANES:(g * pairs + p + 1) * LANES] for p in range(pairs)], axis=0)
            s = _dot_nt(qg, kk)
            probs, sink_logits = [], []
            for p in range(pairs):
                sp = s[p * blk:(p + 1) * blk, :]
                row_probs, rel = [], []
                for half in range(2):
                    sink = sinks_ref[layer, g * group + 2 * p + half]
                    comb = jnp.where(upper, sp[:, 2 * half * blk:(2 * half + 1) * blk],
                                     sp[:, (2 * half + 1) * blk:(2 * half + 2) * blk])
                    if sb == 0:
                        comb = comb + first_bias
                    m = jnp.maximum(jnp.max(comb, axis=-1, keepdims=True), sink)
                    e = jnp.exp(comb - m)
                    rel.append(sink - m)
                    row_probs += [jnp.where(upper, e, 0.0).astype(BF16), jnp.where(upper, 0.0, e).astype(BF16)]
                probs.append(jnp.concatenate(row_probs, axis=1))
                sink_logits.append(jnp.where(low_lanes, rel[0], rel[1]))
            o = _dot(jnp.concatenate(probs, axis=0), jnp.concatenate([vv, head_ind], axis=1))
            for p in range(pairs):
                prs = slice(p * blk, (p + 1) * blk)
                denom = o[prs, LANES:] + jnp.exp(sink_logits[p])
                j = g * pairs + p
                o_ref[rows, j * LANES:(j + 1) * LANES] = (o[prs, :LANES] * (1.0 / denom)).astype(o_ref.dtype)


def _swa(q, kv, sinks, *, layer, n_sub):
    n_batch, seq, width = q.shape
    blk = WINDOW
    n_q_heads = width // HEAD_DIM
    assert 2 * HEAD_DIM == LANES and N_KV_HEADS == 2 and kv.shape[-1] == 4 * LANES
    assert seq % (n_sub * blk) == 0
    prev = lambda b, n: (b, jnp.maximum(n_sub * n - 1, 0), 0)
    cur = lambda b, n: (b, n, 0)
    return pl.pallas_call(
        functools.partial(_swa_kernel, layer=layer, n_q_heads=n_q_heads, blk=blk),
        grid=(n_batch, seq // (n_sub * blk)),
        in_specs=[
            pl.BlockSpec(memory_space=pltpu.SMEM),
            pl.BlockSpec((None, n_sub * blk, width), cur),
            pl.BlockSpec((None, blk, kv.shape[-1]), prev),
            pl.BlockSpec((None, n_sub * blk, kv.shape[-1]), cur),
        ],
        out_specs=pl.BlockSpec((None, n_sub * blk, width), cur),
        out_shape=jax.ShapeDtypeStruct((n_batch, seq, width), BF16),
        compiler_params=_params(2),
        name="swa",
    )(sinks, q, kv, kv)


def _merge_ln_kernel(h_ref, yr_ref, ya_ref, wg_ref, wbr_ref, wba_ref, wo_ref, g_ref, b_ref, o_ref,
                     merged_s):
    d_model = h_ref.shape[1]
    h = h_ref[...]
    hb = h.astype(BF16)
    yr = yr_ref[...]
    ya = ya_ref[...]
    for c in range(d_model // MXU_DIM):
        cs = slice(c * MXU_DIM, (c + 1) * MXU_DIM)
        cs2 = slice(d_model + c * MXU_DIM, d_model + (c + 1) * MXU_DIM)
        gate_r = _sigmoid(_dot(hb, wg_ref[:, cs]))
        gate_a = _sigmoid(_dot(hb, wg_ref[:, cs2]))
        merged = gate_r * _dot(yr, wbr_ref[:, cs]) + gate_a * _dot(ya, wba_ref[:, cs])
        merged_s[:, cs] = merged.astype(BF16)
    mix = _dot(merged_s[...], wo_ref[...])
    o_ref[...] = _layer_norm(DEEPNORM_ALPHA * h + mix, g_ref[...], b_ref[...])


def _merge_ln(h2, yr2, ya2, w_g, w_br, w_ba, w_o, ln_g, ln_b, *, layer, tm):
    n_tok, d_model = h2.shape
    tile = lambda: pl.BlockSpec((tm, d_model), lambda i: (i, 0))
    return pl.pallas_call(
        _merge_ln_kernel,
        grid=(n_tok // tm,),
        in_specs=[tile(), tile(), tile(),
                  *[_layer_spec(p, layer) for p in (w_g, w_br, w_ba, w_o, ln_g, ln_b)]],
        out_specs=tile(),
        out_shape=jax.ShapeDtypeStruct((n_tok, d_model), F32),
        scratch_shapes=[pltpu.VMEM((tm, d_model), BF16)],
        compiler_params=_params(1),
        name="merge_ln",
    )(h2, yr2, ya2, w_g, w_br, w_ba, w_o, ln_g, ln_b)


def _cross_ln_kernel(h_ref, mem_ref, wq_ref, wkv_ref, wo_ref, g_ref, b_ref, o_ref, qk_s, vo_s, p_s):
    d_model = h_ref.shape[1]
    mem_len = mem_ref.shape[0]
    head_dim = d_model // CROSS_HEADS
    scale = head_dim ** -0.5

    @pl.when(pl.program_id(1) == 0)
    def _():
        kv = _dot(mem_ref[...].astype(BF16), wkv_ref[...])
        for hd in range(CROSS_HEADS):
            cs = slice(hd * head_dim, (hd + 1) * head_dim)
            ms = slice(hd * mem_len, (hd + 1) * mem_len)
            k = (kv[:, cs] * scale).astype(BF16)
            v = kv[:, d_model + hd * head_dim:d_model + (hd + 1) * head_dim].astype(BF16)
            qk_s[:, ms] = _dot_nt(wq_ref[:, cs], k).astype(BF16)
            vo_s[ms, :] = _dot(v, wo_ref[cs, :]).astype(BF16)

    h = h_ref[...]
    s = _dot(h.astype(BF16), qk_s[...])
    for hd in range(CROSS_HEADS):
        ms = slice(hd * mem_len, (hd + 1) * mem_len)
        sh = s[:, ms]
        e = jnp.exp(sh - jnp.max(sh, axis=-1, keepdims=True))
        p_s[:, ms] = (e * (1.0 / jnp.sum(e, axis=-1, keepdims=True))).astype(BF16)
    out = _dot(p_s[...], vo_s[...])
    o_ref[...] = _layer_norm(DEEPNORM_ALPHA * h + out, g_ref[...], b_ref[...])


def _cross_ln(h, mem, w_q, w_kv, w_o, ln_g, ln_b, *, layer, tm):
    n_batch, seq, d_model = h.shape
    mem_len = mem.shape[1]
    tile = pl.BlockSpec((None, tm, d_model), lambda b, i: (b, i, 0))
    return pl.pallas_call(
        _cross_ln_kernel,
        grid=(n_batch, seq // tm),
        in_specs=[tile,
                  pl.BlockSpec((None, mem_len, d_model), lambda b, i: (b, 0, 0)),
                  *[_layer_spec(p, layer) for p in (w_q, w_kv, w_o, ln_g, ln_b)]],
        out_specs=tile,
        out_shape=jax.ShapeDtypeStruct((n_batch, seq, d_model), F32),
        scratch_shapes=[pltpu.VMEM((d_model, CROSS_HEADS * mem_len), BF16),
                        pltpu.VMEM((CROSS_HEADS * mem_len, d_model), BF16),
                        pltpu.VMEM((tm, CROSS_HEADS * mem_len), BF16)],
        compiler_params=_params(2),
        name="cross_ln",
    )(h, mem, w_q, w_kv, w_o, ln_g, ln_b)


def _ffn_ln_kernel(h_ref, wi_ref, wo_ref, g_ref, b_ref, o_ref, act_s):
    d_ff = wo_ref.shape[0]
    h = h_ref[...]
    hb = h.astype(BF16)
    for c in range(d_ff // MXU_DIM):
        cs = slice(c * MXU_DIM, (c + 1) * MXU_DIM)
        cs_up = slice(d_ff + c * MXU_DIM, d_ff + (c + 1) * MXU_DIM)
        gate = _dot(hb, wi_ref[:, cs])
        up = _dot(hb, wi_ref[:, cs_up])
        act_s[:, cs] = (jax.nn.silu(gate) * up).astype(BF16)
    out = _dot(act_s[...], wo_ref[...])
    o_ref[...] = _layer_norm(DEEPNORM_ALPHA * h + out, g_ref[...], b_ref[...])


def _ffn_ln(h2, w_i, w_o, ln_g, ln_b, *, layer, tm):
    n_tok, d_model = h2.shape
    d_ff = w_o.shape[1]
    assert d_ff % MXU_DIM == 0
    tile = lambda: pl.BlockSpec((tm, d_model), lambda i: (i, 0))
    return pl.pallas_call(
        _ffn_ln_kernel,
        grid=(n_tok // tm,),
        in_specs=[tile(), *[_layer_spec(p, layer) for p in (w_i, w_o, ln_g, ln_b)]],
        out_specs=tile(),
        out_shape=jax.ShapeDtypeStruct((n_tok, d_model), F32),
        scratch_shapes=[pltpu.VMEM((tm, d_ff), BF16)],
        compiler_params=_params(1),
        name="ffn_ln",
    )(h2, w_i, w_o, ln_g, ln_b)


def _tile_sizes(seq):
    ts = 128 if seq % 128 == 0 else seq
    tm = 1024 if seq % 1024 == 0 else seq
    return ts, tm


def kernel(x, mem, w_in, conv_w, conv_b, w_rg, b_rg, w_ig, b_ig, lru_lambda, w_br_rnn, w_br_attn, sinks, w_out, ln1_g, ln1_b, cq_w, ckv_w, co_w, ln2_g, ln2_b, ffn_wi, ffn_wo, ln3_g, ln3_b):
    n_batch, seq, d_model = x.shape
    depth = w_in.shape[0]
    n_tok = n_batch * seq
    kv_width = N_KV_HEADS * HEAD_DIM
    n_a = 3 * d_model + 2 * kv_width
    ts, tm = _tile_sizes(seq)
    rope_tabs = _rope_tables(seq)
    rows = lambda p: p.reshape(depth, 1, -1)

    w_a = w_in[:, :, :n_a].astype(BF16)
    w_g = w_in[:, :, n_a:].astype(BF16)
    w_gate = jnp.concatenate([w_rg, w_ig], axis=-1).astype(BF16)
    w_br, w_ba, w_o = w_br_rnn.astype(BF16), w_br_attn.astype(BF16), w_out.astype(BF16)
    w_cq, w_ckv, w_co = cq_w.astype(BF16), ckv_w.astype(BF16), co_w.astype(BF16)
    w_fi, w_fo = ffn_wi.astype(BF16), ffn_wo.astype(BF16)

    h = x
    for l in range(depth):
        y_rnn, q, kv = _rglru_qkv(h, w_a, w_gate, conv_w, rows(conv_b), rows(b_rg), rows(b_ig),
                                    rows(lru_lambda), rope_tabs, layer=l, ts=ts)
        y_attn = _swa(q, kv, sinks, layer=l, n_sub=8 if seq % (8 * WINDOW) == 0 else 1)
        h2 = _merge_ln(h.reshape(n_tok, d_model), y_rnn.reshape(n_tok, d_model),
                       y_attn.reshape(n_tok, d_model), w_g, w_br, w_ba, w_o,
                       rows(ln1_g), rows(ln1_b), layer=l, tm=tm)
        h = _cross_ln(h2.reshape(n_batch, seq, d_model), mem, w_cq, w_ckv, w_co,
                      rows(ln2_g), rows(ln2_b), layer=l, tm=tm)
        h2 = _ffn_ln(h.reshape(n_tok, d_model), w_fi, w_fo, rows(ln3_g), rows(ln3_b), layer=l, tm=tm)
        h = h2.reshape(n_batch, seq, d_model)
    return h
```

```python
import functools
import math

import jax
import jax.numpy as jnp
from jax import lax
from jax.experimental import pallas as pl
from jax.experimental.pallas import tpu as pltpu

F32 = jnp.float32
BF16 = jnp.bfloat16

DEPTH = 4
RNN_BLOCKS = 4
CONV_WIDTH = 4
LRU_C = 8.0
HEAD_DIM = 64
N_KV_HEADS = 2
WINDOW = 128
ROPE_THETA = 500000.0
ROT_DIM = HEAD_DIM // 4
CROSS_HEADS = 4
LN_EPS = 1e-5
DEEPNORM_ALPHA = (2 * DEPTH) ** 0.25
NEG_INF = -1e30

SUBLANES = 8
LANES = 128
MXU_DIM = 256
VMEM_LIMIT_BYTES = 56 * 1024 * 1024


def _params(n_grid_dims):
    return pltpu.CompilerParams(
        dimension_semantics=("arbitrary",) * n_grid_dims,
        vmem_limit_bytes=VMEM_LIMIT_BYTES,
    )


def _layer_spec(stack, layer):
    zeros = (0,) * (stack.ndim - 1)
    return pl.BlockSpec((None,) + stack.shape[1:], lambda *_: (layer,) + zeros,
                        pipeline_mode=pl.Buffered(1))


def _next_layer_cast(stacks, layer, n_steps, step_of):
    in_specs, out_specs, out_shapes = [], [], []
    for w in stacks:
        _, k, n = w.shape
        assert k % (n_steps * 2 * SUBLANES) == 0
        slab = k // n_steps
        in_specs.append(pl.BlockSpec((None, slab, n), lambda *g: (layer, step_of(*g), 0)))
        out_specs.append(pl.BlockSpec((None, slab, n), lambda *g: (0, step_of(*g), 0)))
        out_shapes.append(jax.ShapeDtypeStruct((1, k, n), BF16))
    return in_specs, out_specs, out_shapes


def _layer_norm(x, g, b):
    mu = jnp.mean(x, axis=-1, keepdims=True)
    xc = x - mu
    var = jnp.mean(xc * xc, axis=-1, keepdims=True)
    return xc * lax.rsqrt(var + LN_EPS) * g + b


def _sigmoid(x):
    return 0.5 * jnp.tanh(0.5 * x) + 0.5


def _sqrt_nonneg(x):
    return jnp.where(x > 0.0, x * lax.rsqrt(x), 0.0)


def _dot(a, b):
    return jnp.dot(a, b, preferred_element_type=F32)


def _dot_nt(a, b):
    return lax.dot_general(a, b, (((1,), (1,)), ((), ())), preferred_element_type=F32)


def _rglru_qkv_kernel(x_ref, w_ref, wgate_ref, convw_ref, convb_ref, brg_ref, big_ref, lam_ref,
                      rc_ref, rs1_ref, rs2_ref,
                      y_ref, q_ref, kv_ref,
                      tm_s, stage_s, ab_s, convc_s, hc_s, *, n_batch, ts, d_model, kv_width):
    rows = n_batch * ts
    rnn_block = d_model // RNN_BLOCKS
    tail = (CONV_WIDTH - 1) * n_batch
    pitch = _batch_pitch(ts)

    @pl.when(pl.program_id(0) == 0)
    def _():
        convc_s[...] = jnp.zeros_like(convc_s)
        hc_s[...] = jnp.zeros_like(hc_s)

    xs = jnp.concatenate([x_ref[b] for b in range(n_batch)], axis=0).astype(BF16)

    def to_time_major(val, slot):
        n_blk = val.shape[1] // LANES
        for j in range(n_blk):
            for b in range(n_batch):
                tm_s[slot * n_blk + j, b * pitch:b * pitch + ts, :] = (
                    val[b * ts:(b + 1) * ts, j * LANES:(j + 1) * LANES])
        return jnp.concatenate(
            [jnp.concatenate([tm_s[slot * n_blk + j, pl.ds(t, n_batch, stride=pitch), :]
                              for j in range(n_blk)], axis=1)
             for t in range(ts)], axis=0)

    rc, rs1, rs2 = rc_ref[...], rs1_ref[...], rs2_ref[...]
    half = ROT_DIM // 2

    def rope(tile, tabs):
        c, s1, s2 = tabs
        return (tile * c + pltpu.roll(tile, LANES - half, axis=1) * s1
                + pltpu.roll(tile, half, axis=1) * s2)

    q_off = 2 * d_model
    scale = HEAD_DIM ** -0.5
    k_tabs = (rc, rs1, rs2)
    q_tabs = (rc * scale, rs1 * scale, rs2 * scale)

    def emit_q(c):
        qc = _dot(xs, w_ref[:, q_off + c * MXU_DIM:q_off + (c + 1) * MXU_DIM])
        for b in range(n_batch):
            qb = qc[b * ts:(b + 1) * ts, :]
            parts = [rope(qb[:, j * LANES:(j + 1) * LANES], q_tabs) for j in range(MXU_DIM // LANES)]
            q_ref[b, :, c * MXU_DIM:(c + 1) * MXU_DIM] = jnp.concatenate(parts, axis=1).astype(BF16)

    def emit_kv():
        kv_off = 3 * d_model
        kv = _dot(xs, w_ref[:, kv_off:kv_off + 2 * kv_width])
        for b in range(n_batch):
            kvb = kv[b * ts:(b + 1) * ts, :]
            kb = rope(kvb[:, :kv_width], k_tabs)
            vb = kvb[:, kv_width:]
            kv_ref[b] = jnp.concatenate(
                [kb, pltpu.roll(kb, HEAD_DIM, axis=1), vb, pltpu.roll(vb, HEAD_DIM, axis=1)],
                axis=1).astype(BF16)

    neg_c_softplus = -LRU_C * jax.nn.softplus(-lam_ref[...])

    for n in range(RNN_BLOCKS):
        cs = slice(n * rnn_block, (n + 1) * rnn_block)
        xr = to_time_major(_dot(xs, w_ref[:, cs]), n)
        hist = convc_s[:, cs]
        ext = jnp.concatenate([hist, xr], axis=0)
        convc_s[:, cs] = xr[rows - tail:, :]
        xc = convb_ref[:, cs]
        for kk in range(CONV_WIDTH):
            xc = xc + ext[kk * n_batch:kk * n_batch + rows, :] * convw_ref[kk:kk + 1, cs]
        gates = _dot(xc.astype(BF16), wgate_ref[n])
        emit_q(n)
        r = _sigmoid(gates[:, :rnn_block] + brg_ref[:, cs])
        i = _sigmoid(gates[:, rnn_block:] + big_ref[:, cs])
        log_a = neg_c_softplus[:, cs] * r
        a = jnp.exp(log_a)
        mult = _sqrt_nonneg(-jnp.tanh(log_a) * (a * a + 1.0))
        ab_s[0, :, cs] = a
        ab_s[1, :, cs] = mult * (i * xc)

    h = hc_s[...]
    for t in range(ts):
        rs = slice(t * n_batch, (t + 1) * n_batch)
        h = ab_s[0, rs, :] * h + ab_s[1, rs, :]
        for j in range(d_model // LANES):
            stage_s[j, rs, :] = h[:, j * LANES:(j + 1) * LANES]
    hc_s[...] = h

    blk_lanes = rnn_block // LANES
    for n in range(RNN_BLOCKS):
        gate = jax.nn.gelu(
            _dot(xs, w_ref[:, d_model + n * rnn_block:d_model + (n + 1) * rnn_block]))
        for b in range(n_batch):
            hb = jnp.concatenate(
                [stage_s[n * blk_lanes + j, pl.ds(b, ts, stride=n_batch), :] for j in range(blk_lanes)],
                axis=1)
            y_ref[b, :, n * rnn_block:(n + 1) * rnn_block] = (
                hb * gate[b * ts:(b + 1) * ts, :]).astype(BF16)
    emit_kv()


def _batch_pitch(ts):
    groups = ts // SUBLANES
    return SUBLANES * (groups + 1 if groups % 2 == 0 else groups)


def _rope_tables(seq_len):
    half = ROT_DIM // 2
    pos = jnp.arange(seq_len, dtype=F32)
    inv_freq = ROPE_THETA ** (-jnp.arange(0, ROT_DIM, 2, dtype=F32) / ROT_DIM)
    ang = pos[:, None] * inv_freq[None, :]
    cos, sin = jnp.cos(ang), jnp.sin(ang)
    ones = jnp.ones((seq_len, HEAD_DIM - ROT_DIM), F32)
    zeros = jnp.zeros((seq_len, HEAD_DIM - ROT_DIM), F32)
    zh = jnp.zeros((seq_len, half), F32)
    c_head = jnp.concatenate([cos, cos, ones], axis=1)
    s1_head = jnp.concatenate([-sin, zh, zeros], axis=1)
    s2_head = jnp.concatenate([zh, sin, zeros], axis=1)
    reps = LANES // HEAD_DIM
    return (jnp.tile(c_head, (1, reps)), jnp.tile(s1_head, (1, reps)), jnp.tile(s2_head, (1, reps)))


def _rglru_qkv(h, w_a, w_gate, conv_w, conv_b, b_rg, b_ig, lam, rope_tabs, *, layer, w_layer, ts):
    n_batch, seq, d_model = h.shape
    kv_width = N_KV_HEADS * HEAD_DIM
    assert n_batch == SUBLANES and seq % ts == 0 and ts % SUBLANES == 0 and kv_width == LANES
    rows = n_batch * ts
    kernel = functools.partial(_rglru_qkv_kernel, n_batch=n_batch, ts=ts, d_model=d_model,
                               kv_width=kv_width)
    tile3 = lambda width: pl.BlockSpec((n_batch, ts, width), lambda i: (0, i, 0))
    rope_spec = pl.BlockSpec((ts, LANES), lambda i: (i, 0))
    return pl.pallas_call(
        kernel,
        grid=(seq // ts,),
        in_specs=[
            tile3(d_model),
            _layer_spec(w_a, w_layer),
            *[_layer_spec(p, layer) for p in (w_gate, conv_w, conv_b, b_rg, b_ig, lam)],
            rope_spec, rope_spec, rope_spec,
        ],
        out_specs=[tile3(d_model), tile3(d_model), tile3(4 * kv_width)],
        out_shape=[
            jax.ShapeDtypeStruct((n_batch, seq, d_model), BF16),
            jax.ShapeDtypeStruct((n_batch, seq, d_model), BF16),
            jax.ShapeDtypeStruct((n_batch, seq, 4 * kv_width), BF16),
        ],
        scratch_shapes=[
            pltpu.VMEM((d_model // LANES, n_batch * _batch_pitch(ts), LANES), F32),
            pltpu.VMEM((d_model // LANES, rows, LANES), F32),
            pltpu.VMEM((2, rows, d_model), F32),
            pltpu.VMEM(((CONV_WIDTH - 1) * n_batch, d_model), F32),
            pltpu.VMEM((n_batch, d_model), F32),
        ],
        compiler_params=_params(1),
        name="rglru_qkv",
    )(h, w_a, w_gate, conv_w, conv_b, b_rg, b_ig, lam, *rope_tabs)


def _swa_kernel(sinks_ref, q_ref, kvp_ref, kvc_ref, o_ref, *, layer, n_q_heads, blk):
    n_sub = q_ref.shape[0] // blk
    group = n_q_heads // N_KV_HEADS
    pairs = group // 2
    row = lax.broadcasted_iota(jnp.int32, (blk, blk), 0)
    col = lax.broadcasted_iota(jnp.int32, (blk, blk), 1)
    upper = col > row
    no_prev = jnp.where(pl.program_id(1) == 0, NEG_INF, 0.0)
    first_bias = jnp.where(upper, no_prev, 0.0)
    low_lanes = lax.broadcasted_iota(jnp.int32, (blk, LANES), 1) < HEAD_DIM
    low2 = lax.broadcasted_iota(jnp.int32, (2 * blk, LANES), 1) < HEAD_DIM
    head_ind = jnp.concatenate([jnp.where(low2, 1.0, 0.0), jnp.where(low2, 0.0, 1.0)],
                               axis=0).astype(BF16)

    def keep(x, low):
        return jnp.where(low_lanes if low else ~low_lanes, x, jnp.zeros_like(x))

    for sb in range(n_sub):
        rows = slice(sb * blk, (sb + 1) * blk)

        def prev(j):
            if sb == 0:
                return kvp_ref[:, j * LANES:(j + 1) * LANES]
            return kvc_ref[(sb - 1) * blk:sb * blk, j * LANES:(j + 1) * LANES]

        def cur(j):
            return kvc_ref[rows, j * LANES:(j + 1) * LANES]

        for g in range(N_KV_HEADS):
            low_src, high_src = (0, 1) if g == 0 else (1, 0)
            kk = jnp.concatenate([keep(prev(low_src), True), keep(cur(low_src), True),
                                  keep(prev(high_src), False), keep(cur(high_src), False)], axis=0)
            vv = jnp.concatenate([keep(prev(2 + low_src), True), keep(cur(2 + low_src), True),
                                  keep(prev(2 + high_src), False), keep(cur(2 + high_src), False)], axis=0)
            qg = jnp.concatenate(
                [q_ref[rows, (g * pairs + p) * LANES:(g * pairs + p + 1) * LANES] for p in range(pairs)], axis=0)
            s = _dot_nt(qg, kk)
            probs, sink_logits = [], []
            for p in range(pairs):
                sp = s[p * blk:(p + 1) * blk, :]
                row_probs, rel = [], []
                for half in range(2):
                    sink = sinks_ref[layer, g * group + 2 * p + half]
                    comb = jnp.where(upper, sp[:, 2 * half * blk:(2 * half + 1) * blk],
                                     sp[:, (2 * half + 1) * blk:(2 * half + 2) * blk])
                    if sb == 0:
                        comb = comb + first_bias
                    m = jnp.maximum(jnp.max(comb, axis=-1, keepdims=True), sink)
                    e = jnp.exp(comb - m)
                    rel.append(sink - m)
                    row_probs += [jnp.where(upper, e, 0.0).astype(BF16), jnp.where(upper, 0.0, e).astype(BF16)]
                probs.append(jnp.concatenate(row_probs, axis=1))
                sink_logits.append(jnp.where(low_lanes, rel[0], rel[1]))
            o = _dot(jnp.concatenate(probs, axis=0), jnp.concatenate([vv, head_ind], axis=1))
            for p in range(pairs):
                prs = slice(p * blk, (p + 1) * blk)
                denom = o[prs, LANES:] + jnp.exp(sink_logits[p])
                j = g * pairs + p
                o_ref[rows, j * LANES:(j + 1) * LANES] = (o[prs, :LANES] * (1.0 / denom)).astype(o_ref.dtype)


def _swa(q, kv, sinks, *, layer, n_sub):
    n_batch, seq, width = q.shape
    blk = WINDOW
    n_q_heads = width // HEAD_DIM
    assert 2 * HEAD_DIM == LANES and N_KV_HEADS == 2 and kv.shape[-1] == 4 * LANES
    assert seq % (n_sub * blk) == 0
    prev = lambda b, n: (b, jnp.maximum(n_sub * n - 1, 0), 0)
    cur = lambda b, n: (b, n, 0)
    return pl.pallas_call(
        functools.partial(_swa_kernel, layer=layer, n_q_heads=n_q_heads, blk=blk),
        grid=(n_batch, seq // (n_sub * blk)),
        in_specs=[
            pl.BlockSpec(memory_space=pltpu.SMEM),
            pl.BlockSpec((None, n_sub * blk, width), cur),
            pl.BlockSpec((None, blk, kv.shape[-1]), prev),
            pl.BlockSpec((None, n_sub * blk, kv.shape[-1]), cur),
        ],
        out_specs=pl.BlockSpec((None, n_sub * blk, width), cur),
        out_shape=jax.ShapeDtypeStruct((n_batch, seq, width), BF16),
        compiler_params=_params(2),
        name="swa",
    )(sinks, q, kv, kv)


def _merge_ln_kernel(h_ref, yr_ref, ya_ref, win_ref, wbr_ref, wba_ref, wo_ref, g_ref, b_ref, o_ref,
                     merged_s, *, gate_off):
    d_model = h_ref.shape[1]
    h = h_ref[...]
    hb = h.astype(BF16)
    yr = yr_ref[...]
    ya = ya_ref[...]
    for c in range(d_model // MXU_DIM):
        cs = slice(c * MXU_DIM, (c + 1) * MXU_DIM)
        gs_r = slice(gate_off + c * MXU_DIM, gate_off + (c + 1) * MXU_DIM)
        gs_a = slice(gate_off + d_model + c * MXU_DIM, gate_off + d_model + (c + 1) * MXU_DIM)
        gate_r = _sigmoid(_dot(hb, win_ref[:, gs_r]))
        gate_a = _sigmoid(_dot(hb, win_ref[:, gs_a]))
        merged = gate_r * _dot(yr, wbr_ref[:, cs]) + gate_a * _dot(ya, wba_ref[:, cs])
        merged_s[:, cs] = merged.astype(BF16)
    mix = _dot(merged_s[...], wo_ref[...])
    o_ref[...] = _layer_norm(DEEPNORM_ALPHA * h + mix, g_ref[...], b_ref[...])


def _merge_ln(h2, yr2, ya2, w_in, w_br, w_ba, w_o, ln_g, ln_b, *, layer, w_layer, gate_off, tm):
    n_tok, d_model = h2.shape
    tile = lambda: pl.BlockSpec((tm, d_model), lambda i: (i, 0))
    return pl.pallas_call(
        functools.partial(_merge_ln_kernel, gate_off=gate_off),
        grid=(n_tok // tm,),
        in_specs=[tile(), tile(), tile(), _layer_spec(w_in, w_layer),
                  *[_layer_spec(p, layer) for p in (w_br, w_ba, w_o, ln_g, ln_b)]],
        out_specs=tile(),
        out_shape=jax.ShapeDtypeStruct((n_tok, d_model), F32),
        scratch_shapes=[pltpu.VMEM((tm, d_model), BF16)],
        compiler_params=_params(1),
        name="merge_ln",
    )(h2, yr2, ya2, w_in, w_br, w_ba, w_o, ln_g, ln_b)


def _cross_ln_kernel(h_ref, mem_ref, wq_ref, wkv_ref, wo_ref, g_ref, b_ref, *refs, n_casts):
    cast_in, o_ref, cast_out = refs[:n_casts], refs[n_casts], refs[n_casts + 1:2 * n_casts + 1]
    qk_s, vo_s, p_s = refs[2 * n_casts + 1:]
    for src, dst in zip(cast_in, cast_out):
        dst[...] = src[...].astype(BF16)
    d_model = h_ref.shape[1]
    mem_len = mem_ref.shape[0]
    head_dim = d_model // CROSS_HEADS
    scale = head_dim ** -0.5

    @pl.when(pl.program_id(1) == 0)
    def _():
        kv = _dot(mem_ref[...].astype(BF16), wkv_ref[...])
        for hd in range(CROSS_HEADS):
            cs = slice(hd * head_dim, (hd + 1) * head_dim)
            ms = slice(hd * mem_len, (hd + 1) * mem_len)
            k = (kv[:, cs] * scale).astype(BF16)
            v = kv[:, d_model + hd * head_dim:d_model + (hd + 1) * head_dim].astype(BF16)
            qk_s[:, ms] = _dot_nt(wq_ref[:, cs], k).astype(BF16)
            vo_s[ms, :] = _dot(v, wo_ref[cs, :]).astype(BF16)

    h = h_ref[...]
    s = _dot(h.astype(BF16), qk_s[...])
    for hd in range(CROSS_HEADS):
        ms = slice(hd * mem_len, (hd + 1) * mem_len)
        sh = s[:, ms]
        e = jnp.exp(sh - jnp.max(sh, axis=-1, keepdims=True))
        p_s[:, ms] = (e * (1.0 / jnp.sum(e, axis=-1, keepdims=True))).astype(BF16)
    out = _dot(p_s[...], vo_s[...])
    o_ref[...] = _layer_norm(DEEPNORM_ALPHA * h + out, g_ref[...], b_ref[...])


def _cross_ln(h, mem, w_q, w_kv, w_o, ln_g, ln_b, *, layer, tm, cast_stacks=(), cast_layer=None):
    n_batch, seq, d_model = h.shape
    mem_len = mem.shape[1]
    per_batch = seq // tm
    tile = pl.BlockSpec((None, tm, d_model), lambda b, i: (b, i, 0))
    cast_in, cast_out, cast_shapes = _next_layer_cast(
        cast_stacks, cast_layer, n_batch * per_batch, lambda b, i: b * per_batch + i)
    out, *casts = pl.pallas_call(
        functools.partial(_cross_ln_kernel, n_casts=len(cast_stacks)),
        grid=(n_batch, per_batch),
        in_specs=[tile,
                  pl.BlockSpec((None, mem_len, d_model), lambda b, i: (b, 0, 0)),
                  *[_layer_spec(p, layer) for p in (w_q, w_kv, w_o, ln_g, ln_b)], *cast_in],
        out_specs=[tile, *cast_out],
        out_shape=[jax.ShapeDtypeStruct((n_batch, seq, d_model), F32), *cast_shapes],
        scratch_shapes=[pltpu.VMEM((d_model, CROSS_HEADS * mem_len), BF16),
                        pltpu.VMEM((CROSS_HEADS * mem_len, d_model), BF16),
                        pltpu.VMEM((tm, CROSS_HEADS * mem_len), BF16)],
        compiler_params=_params(2),
        name="cross_ln",
    )(h, mem, w_q, w_kv, w_o, ln_g, ln_b, *cast_stacks)
    return out, casts


def _ffn_ln_kernel(h_ref, wi_ref, wo_ref, g_ref, b_ref, *refs, n_casts):
    cast_in, o_ref, cast_out = refs[:n_casts], refs[n_casts], refs[n_casts + 1:2 * n_casts + 1]
    act_s, = refs[2 * n_casts + 1:]
    for src, dst in zip(cast_in, cast_out):
        dst[...] = src[...].astype(BF16)
    d_ff = wo_ref.shape[0]
    h = h_ref[...]
    hb = h.astype(BF16)
    for c in range(d_ff // MXU_DIM):
        cs = slice(c * MXU_DIM, (c + 1) * MXU_DIM)
        cs_up = slice(d_ff + c * MXU_DIM, d_ff + (c + 1) * MXU_DIM)
        gate = _dot(hb, wi_ref[:, cs])
        up = _dot(hb, wi_ref[:, cs_up])
        act_s[:, cs] = (jax.nn.silu(gate) * up).astype(BF16)
    out = _dot(act_s[...], wo_ref[...])
    o_ref[...] = _layer_norm(DEEPNORM_ALPHA * h + out, g_ref[...], b_ref[...])


def _ffn_ln(h2, w_i, w_o, ln_g, ln_b, *, layer, w_layer, tm, cast_stacks=(), cast_layer=None):
    n_tok, d_model = h2.shape
    d_ff = w_o.shape[1]
    assert d_ff % MXU_DIM == 0
    n_tiles = n_tok // tm
    tile = lambda: pl.BlockSpec((tm, d_model), lambda i: (i, 0))
    cast_in, cast_out, cast_shapes = _next_layer_cast(cast_stacks, cast_layer, n_tiles, lambda i: i)
    out, *casts = pl.pallas_call(
        functools.partial(_ffn_ln_kernel, n_casts=len(cast_stacks)),
        grid=(n_tiles,),
        in_specs=[tile(), _layer_spec(w_i, w_layer), _layer_spec(w_o, w_layer),
                  _layer_spec(ln_g, layer), _layer_spec(ln_b, layer), *cast_in],
        out_specs=[tile(), *cast_out],
        out_shape=[jax.ShapeDtypeStruct((n_tok, d_model), F32), *cast_shapes],
        scratch_shapes=[pltpu.VMEM((tm, d_ff), BF16)],
        compiler_params=_params(1),
        name="ffn_ln",
    )(h2, w_i, w_o, ln_g, ln_b, *cast_stacks)
    return out, casts


def _tile_sizes(seq):
    ts = 128 if seq % 128 == 0 else seq
    tm = 1024 if seq % 1024 == 0 else seq
    return ts, tm


def kernel(x, mem, w_in, conv_w, conv_b, w_rg, b_rg, w_ig, b_ig, lru_lambda, w_br_rnn, w_br_attn, sinks, w_out, ln1_g, ln1_b, cq_w, ckv_w, co_w, ln2_g, ln2_b, ffn_wi, ffn_wo, ln3_g, ln3_b):
    n_batch, seq, d_model = x.shape
    depth = w_in.shape[0]
    n_tok = n_batch * seq
    kv_width = N_KV_HEADS * HEAD_DIM
    n_a = 3 * d_model + 2 * kv_width
    ts, tm = _tile_sizes(seq)
    rope_tabs = _rope_tables(seq)
    rows = lambda p: p.reshape(depth, 1, -1)

    w_gate = jnp.concatenate([w_rg, w_ig], axis=-1).astype(BF16)
    w_br, w_ba, w_o = w_br_rnn.astype(BF16), w_br_attn.astype(BF16), w_out.astype(BF16)
    w_cq, w_ckv, w_co = cq_w.astype(BF16), ckv_w.astype(BF16), co_w.astype(BF16)
    w_in_b, w_fi, w_fo = (w[:1].astype(BF16) for w in (w_in, ffn_wi, ffn_wo))

    h = x
    for l in range(depth):
        has_next = l + 1 < depth
        y_rnn, q, kv = _rglru_qkv(h, w_in_b, w_gate, conv_w, rows(conv_b), rows(b_rg), rows(b_ig),
                                    rows(lru_lambda), rope_tabs, layer=l, w_layer=0, ts=ts)
        y_attn = _swa(q, kv, sinks, layer=l, n_sub=8 if seq % (8 * WINDOW) == 0 else 1)
        h2 = _merge_ln(h.reshape(n_tok, d_model), y_rnn.reshape(n_tok, d_model),
                       y_attn.reshape(n_tok, d_model), w_in_b, w_br, w_ba, w_o,
                       rows(ln1_g), rows(ln1_b), layer=l, w_layer=0, gate_off=n_a, tm=tm)
        h, casts = _cross_ln(h2.reshape(n_batch, seq, d_model), mem, w_cq, w_ckv, w_co,
                             rows(ln2_g), rows(ln2_b), layer=l, tm=tm,
                             cast_stacks=(w_in,) if has_next else (), cast_layer=l + 1)
        if has_next:
            w_in_b, = casts
        h2, casts = _ffn_ln(h.reshape(n_tok, d_model), w_fi, w_fo, rows(ln3_g), rows(ln3_b),
                            layer=l, w_layer=0, tm=tm,
                            cast_stacks=(ffn_wi, ffn_wo) if has_next else (), cast_layer=l + 1)
        if has_next:
            w_fi, w_fo = casts
        h = h2.reshape(n_batch, seq, d_model)
    return h
```

```python
import functools
import math

import jax
import jax.numpy as jnp
from jax import lax
from jax.experimental import pallas as pl
from jax.experimental.pallas import tpu as pltpu

F32 = jnp.float32
BF16 = jnp.bfloat16

DEPTH = 4
RNN_BLOCKS = 4
CONV_WIDTH = 4
LRU_C = 8.0
HEAD_DIM = 64
N_KV_HEADS = 2
WINDOW = 128
ROPE_THETA = 500000.0
ROT_DIM = HEAD_DIM // 4
CROSS_HEADS = 4
LN_EPS = 1e-5
DEEPNORM_ALPHA = (2 * DEPTH) ** 0.25
NEG_INF = -1e30

SUBLANES = 8
LANES = 128
MXU_DIM = 256
VMEM_LIMIT_BYTES = 56 * 1024 * 1024


def _params(n_grid_dims):
    return pltpu.CompilerParams(
        dimension_semantics=("arbitrary",) * n_grid_dims,
        vmem_limit_bytes=VMEM_LIMIT_BYTES,
    )


def _layer_spec(stack, layer):
    zeros = (0,) * (stack.ndim - 1)
    return pl.BlockSpec((None,) + stack.shape[1:], lambda *_: (layer,) + zeros,
                        pipeline_mode=pl.Buffered(1))


def _next_layer_cast(stacks, layer, n_steps, step_of):
    in_specs, out_specs, out_shapes = [], [], []
    for w in stacks:
        _, k, n = w.shape
        assert k % (n_steps * 2 * SUBLANES) == 0
        slab = k // n_steps
        in_specs.append(pl.BlockSpec((None, slab, n), lambda *g: (layer, step_of(*g), 0)))
        out_specs.append(pl.BlockSpec((None, slab, n), lambda *g: (0, step_of(*g), 0)))
        out_shapes.append(jax.ShapeDtypeStruct((1, k, n), BF16))
    return in_specs, out_specs, out_shapes


def _cast_kernel(*refs):
    n = len(refs) // 2
    for src, dst in zip(refs[:n], refs[n:]):
        dst[...] = src[...].astype(BF16)


def _cast_first_layer(stacks, n_steps=8):
    in_specs, out_specs, out_shapes = _next_layer_cast(stacks, 0, n_steps, lambda i: i)
    return pl.pallas_call(
        _cast_kernel,
        grid=(n_steps,),
        in_specs=in_specs,
        out_specs=out_specs,
        out_shape=out_shapes,
        compiler_params=_params(1),
        name="cast_first_layer",
    )(*stacks)


def _layer_norm(x, g, b):
    mu = jnp.mean(x, axis=-1, keepdims=True)
    xc = x - mu
    var = jnp.mean(xc * xc, axis=-1, keepdims=True)
    return xc * lax.rsqrt(var + LN_EPS) * g + b


def _sigmoid(x):
    return 0.5 * jnp.tanh(0.5 * x) + 0.5


def _sqrt_nonneg(x):
    return jnp.where(x > 0.0, x * lax.rsqrt(x), 0.0)


def _dot(a, b):
    return jnp.dot(a, b, preferred_element_type=F32)


def _dot_nt(a, b):
    return lax.dot_general(a, b, (((1,), (1,)), ((), ())), preferred_element_type=F32)


def _rglru_qkv_kernel(x_ref, w_ref, wgate_ref, convw_ref, convb_ref, brg_ref, big_ref, lam_ref,
                      rc_ref, rs1_ref, rs2_ref,
                      y_ref, q_ref, kv_ref,
                      tm_s, stage_s, ab_s, convc_s, hc_s, *, n_batch, ts, d_model, kv_width):
    rows = n_batch * ts
    rnn_block = d_model // RNN_BLOCKS
    tail = (CONV_WIDTH - 1) * n_batch
    pitch = _batch_pitch(ts)

    @pl.when(pl.program_id(0) == 0)
    def _():
        convc_s[...] = jnp.zeros_like(convc_s)
        hc_s[...] = jnp.zeros_like(hc_s)

    xs = jnp.concatenate([x_ref[b] for b in range(n_batch)], axis=0).astype(BF16)

    def to_time_major(val, slot):
        n_blk = val.shape[1] // LANES
        for j in range(n_blk):
            for b in range(n_batch):
                tm_s[slot * n_blk + j, b * pitch:b * pitch + ts, :] = (
                    val[b * ts:(b + 1) * ts, j * LANES:(j + 1) * LANES])
        return jnp.concatenate(
            [jnp.concatenate([tm_s[slot * n_blk + j, pl.ds(t, n_batch, stride=pitch), :]
                              for j in range(n_blk)], axis=1)
             for t in range(ts)], axis=0)

    rc, rs1, rs2 = rc_ref[...], rs1_ref[...], rs2_ref[...]
    half = ROT_DIM // 2

    def rope(tile, tabs):
        c, s1, s2 = tabs
        return (tile * c + pltpu.roll(tile, LANES - half, axis=1) * s1
                + pltpu.roll(tile, half, axis=1) * s2)

    q_off = 2 * d_model
    scale = HEAD_DIM ** -0.5
    k_tabs = (rc, rs1, rs2)
    q_tabs = (rc * scale, rs1 * scale, rs2 * scale)

    def emit_q(c):
        qc = _dot(xs, w_ref[:, q_off + c * MXU_DIM:q_off + (c + 1) * MXU_DIM])
        for b in range(n_batch):
            qb = qc[b * ts:(b + 1) * ts, :]
            parts = [rope(qb[:, j * LANES:(j + 1) * LANES], q_tabs) for j in range(MXU_DIM // LANES)]
            q_ref[b, :, c * MXU_DIM:(c + 1) * MXU_DIM] = jnp.concatenate(parts, axis=1).astype(BF16)

    def emit_kv():
        kv_off = 3 * d_model
        kv = _dot(xs, w_ref[:, kv_off:kv_off + 2 * kv_width])
        for b in range(n_batch):
            kvb = kv[b * ts:(b + 1) * ts, :]
            kb = rope(kvb[:, :kv_width], k_tabs)
            vb = kvb[:, kv_width:]
            kv_ref[b] = jnp.concatenate(
                [kb, pltpu.roll(kb, HEAD_DIM, axis=1), vb, pltpu.roll(vb, HEAD_DIM, axis=1)],
                axis=1).astype(BF16)

    neg_c_softplus = -LRU_C * jax.nn.softplus(-lam_ref[...])

    for n in range(RNN_BLOCKS):
        cs = slice(n * rnn_block, (n + 1) * rnn_block)
        xr = to_time_major(_dot(xs, w_ref[:, cs]), n)
        hist = convc_s[:, cs]
        ext = jnp.concatenate([hist, xr], axis=0)
        convc_s[:, cs] = xr[rows - tail:, :]
        xc = convb_ref[:, cs]
        for kk in range(CONV_WIDTH):
            xc = xc + ext[kk * n_batch:kk * n_batch + rows, :] * convw_ref[kk:kk + 1, cs]
        gates = _dot(xc.astype(BF16), wgate_ref[n])
        emit_q(n)
        r = _sigmoid(gates[:, :rnn_block] + brg_ref[:, cs])
        i = _sigmoid(gates[:, rnn_block:] + big_ref[:, cs])
        log_a = neg_c_softplus[:, cs] * r
        a = jnp.exp(log_a)
        mult = _sqrt_nonneg(-jnp.tanh(log_a) * (a * a + 1.0))
        ab_s[0, :, cs] = a
        ab_s[1, :, cs] = mult * (i * xc)

    h = hc_s[...]
    for t in range(ts):
        rs = slice(t * n_batch, (t + 1) * n_batch)
        h = ab_s[0, rs, :] * h + ab_s[1, rs, :]
        for j in range(d_model // LANES):
            stage_s[j, rs, :] = h[:, j * LANES:(j + 1) * LANES]
    hc_s[...] = h

    blk_lanes = rnn_block // LANES
    for n in range(RNN_BLOCKS):
        gate = jax.nn.gelu(
            _dot(xs, w_ref[:, d_model + n * rnn_block:d_model + (n + 1) * rnn_block]))
        for b in range(n_batch):
            hb = jnp.concatenate(
                [stage_s[n * blk_lanes + j, pl.ds(b, ts, stride=n_batch), :] for j in range(blk_lanes)],
                axis=1)
            y_ref[b, :, n * rnn_block:(n + 1) * rnn_block] = (
                hb * gate[b * ts:(b + 1) * ts, :]).astype(BF16)
    emit_kv()


def _batch_pitch(ts):
    groups = ts // SUBLANES
    return SUBLANES * (groups + 1 if groups % 2 == 0 else groups)


def _rope_tables(seq_len):
    half = ROT_DIM // 2
    pos = jnp.arange(seq_len, dtype=F32)
    inv_freq = ROPE_THETA ** (-jnp.arange(0, ROT_DIM, 2, dtype=F32) / ROT_DIM)
    ang = pos[:, None] * inv_freq[None, :]
    cos, sin = jnp.cos(ang), jnp.sin(ang)
    ones = jnp.ones((seq_len, HEAD_DIM - ROT_DIM), F32)
    zeros = jnp.zeros((seq_len, HEAD_DIM - ROT_DIM), F32)
    zh = jnp.zeros((seq_len, half), F32)
    c_head = jnp.concatenate([cos, cos, ones], axis=1)
    s1_head = jnp.concatenate([-sin, zh, zeros], axis=1)
    s2_head = jnp.concatenate([zh, sin, zeros], axis=1)
    reps = LANES // HEAD_DIM
    return (jnp.tile(c_head, (1, reps)), jnp.tile(s1_head, (1, reps)), jnp.tile(s2_head, (1, reps)))


def _rglru_qkv(h, w_a, w_gate, conv_w, conv_b, b_rg, b_ig, lam, rope_tabs, *, layer, w_layer, ts):
    n_batch, seq, d_model = h.shape
    kv_width = N_KV_HEADS * HEAD_DIM
    assert n_batch == SUBLANES and seq % ts == 0 and ts % SUBLANES == 0 and kv_width == LANES
    rows = n_batch * ts
    kernel = functools.partial(_rglru_qkv_kernel, n_batch=n_batch, ts=ts, d_model=d_model,
                               kv_width=kv_width)
    tile3 = lambda width: pl.BlockSpec((n_batch, ts, width), lambda i: (0, i, 0))
    rope_spec = pl.BlockSpec((ts, LANES), lambda i: (i, 0))
    return pl.pallas_call(
        kernel,
        grid=(seq // ts,),
        in_specs=[
            tile3(d_model),
            _layer_spec(w_a, w_layer),
            *[_layer_spec(p, layer) for p in (w_gate, conv_w, conv_b, b_rg, b_ig, lam)],
            rope_spec, rope_spec, rope_spec,
        ],
        out_specs=[tile3(d_model), tile3(d_model), tile3(4 * kv_width)],
        out_shape=[
            jax.ShapeDtypeStruct((n_batch, seq, d_model), BF16),
            jax.ShapeDtypeStruct((n_batch, seq, d_model), BF16),
            jax.ShapeDtypeStruct((n_batch, seq, 4 * kv_width), BF16),
        ],
        scratch_shapes=[
            pltpu.VMEM((d_model // LANES, n_batch * _batch_pitch(ts), LANES), F32),
            pltpu.VMEM((d_model // LANES, rows, LANES), F32),
            pltpu.VMEM((2, rows, d_model), F32),
            pltpu.VMEM(((CONV_WIDTH - 1) * n_batch, d_model), F32),
            pltpu.VMEM((n_batch, d_model), F32),
        ],
        compiler_params=_params(1),
        name="rglru_qkv",
    )(h, w_a, w_gate, conv_w, conv_b, b_rg, b_ig, lam, *rope_tabs)


def _swa_kernel(sinks_ref, q_ref, kvp_ref, kvc_ref, o_ref, *, layer, n_q_heads, blk):
    n_sub = q_ref.shape[0] // blk
    group = n_q_heads // N_KV_HEADS
    pairs = group // 2
    row = lax.broadcasted_iota(jnp.int32, (blk, blk), 0)
    col = lax.broadcasted_iota(jnp.int32, (blk, blk), 1)
    upper = col > row
    no_prev = jnp.where(pl.program_id(1) == 0, NEG_INF, 0.0)
    first_bias = jnp.where(upper, no_prev, 0.0)
    low_lanes = lax.broadcasted_iota(jnp.int32, (blk, LANES), 1) < HEAD_DIM
    low2 = lax.broadcasted_iota(jnp.int32, (2 * blk, LANES), 1) < HEAD_DIM
    head_ind = jnp.concatenate([jnp.where(low2, 1.0, 0.0), jnp.where(low2, 0.0, 1.0)],
                               axis=0).astype(BF16)

    def keep(x, low):
        return jnp.where(low_lanes if low else ~low_lanes, x, jnp.zeros_like(x))

    for sb in range(n_sub):
        rows = slice(sb * blk, (sb + 1) * blk)

        def prev(j):
            if sb == 0:
                return kvp_ref[:, j * LANES:(j + 1) * LANES]
            return kvc_ref[(sb - 1) * blk:sb * blk, j * LANES:(j + 1) * LANES]

        def cur(j):
            return kvc_ref[rows, j * LANES:(j + 1) * LANES]

        for g in range(N_KV_HEADS):
            low_src, high_src = (0, 1) if g == 0 else (1, 0)
            kk = jnp.concatenate([keep(prev(low_src), True), keep(cur(low_src), True),
                                  keep(prev(high_src), False), keep(cur(high_src), False)], axis=0)
            vv = jnp.concatenate([keep(prev(2 + low_src), True), keep(cur(2 + low_src), True),
                                  keep(prev(2 + high_src), False), keep(cur(2 + high_src), False)], axis=0)
            qg = jnp.concatenate(
                [q_ref[rows, (g * pairs + p) * LANES:(g * pairs + p + 1) * LANES] for p in range(pairs)], axis=0)
            s = _dot_nt(qg, kk)
            probs, sink_logits = [], []
            for p in range(pairs):
                sp = s[p * blk:(p + 1) * blk, :]
                row_probs, rel = [], []
                for half in range(2):
                    sink = sinks_ref[layer, g * group + 2 * p + half]
                    comb = jnp.where(upper, sp[:, 2 * half * blk:(2 * half + 1) * blk],
                                     sp[:, (2 * half + 1) * blk:(2 * half + 2) * blk])
                    if sb == 0:
                        comb = comb + first_bias
                    m = jnp.maximum(jnp.max(comb, axis=-1, keepdims=True), sink)
                    e = jnp.exp(comb - m)
                    rel.append(sink - m)
                    row_probs += [jnp.where(upper, e, 0.0).astype(BF16), jnp.where(upper, 0.0, e).astype(BF16)]
                probs.append(jnp.concatenate(row_probs, axis=1))
                sink_logits.append(jnp.where(low_lanes, rel[0], rel[1]))
            o = _dot(jnp.concatenate(probs, axis=0), jnp.concatenate([vv, head_ind], axis=1))
            for p in range(pairs):
                prs = slice(p * blk, (p + 1) * blk)
                denom = o[prs, LANES:] + jnp.exp(sink_logits[p])
                j = g * pairs + p
                o_ref[rows, j * LANES:(j + 1) * LANES] = (o[prs, :LANES] * (1.0 / denom)).astype(o_ref.dtype)


def _swa(q, kv, sinks, *, layer, n_sub):
    n_batch, seq, width = q.shape
    blk = WINDOW
    n_q_heads = width // HEAD_DIM
    assert 2 * HEAD_DIM == LANES and N_KV_HEADS == 2 and kv.shape[-1] == 4 * LANES
    assert seq % (n_sub * blk) == 0
    prev = lambda b, n: (b, jnp.maximum(n_sub * n - 1, 0), 0)
    cur = lambda b, n: (b, n, 0)
    return pl.pallas_call(
        functools.partial(_swa_kernel, layer=layer, n_q_heads=n_q_heads, blk=blk),
        grid=(n_batch, seq // (n_sub * blk)),
        in_specs=[
            pl.BlockSpec(memory_space=pltpu.SMEM),
            pl.BlockSpec((None, n_sub * blk, width), cur),
            pl.BlockSpec((None, blk, kv.shape[-1]), prev),
            pl.BlockSpec((None, n_sub * blk, kv.shape[-1]), cur),
        ],
        out_specs=pl.BlockSpec((None, n_sub * blk, width), cur),
        out_shape=jax.ShapeDtypeStruct((n_batch, seq, width), BF16),
        compiler_params=_params(2),
        name="swa",
    )(sinks, q, kv, kv)


def _merge_ln_kernel(h_ref, yr_ref, ya_ref, win_ref, wbr_ref, wba_ref, wo_ref, g_ref, b_ref, o_ref,
                     merged_s, *, gate_off):
    d_model = h_ref.shape[1]
    h = h_ref[...]
    hb = h.astype(BF16)
    yr = yr_ref[...]
    ya = ya_ref[...]
    for c in range(d_model // MXU_DIM):
        cs = slice(c * MXU_DIM, (c + 1) * MXU_DIM)
        gs_r = slice(gate_off + c * MXU_DIM, gate_off + (c + 1) * MXU_DIM)
        gs_a = slice(gate_off + d_model + c * MXU_DIM, gate_off + d_model + (c + 1) * MXU_DIM)
        gate_r = _sigmoid(_dot(hb, win_ref[:, gs_r]))
        gate_a = _sigmoid(_dot(hb, win_ref[:, gs_a]))
        merged = gate_r * _dot(yr, wbr_ref[:, cs]) + gate_a * _dot(ya, wba_ref[:, cs])
        merged_s[:, cs] = merged.astype(BF16)
    mix = _dot(merged_s[...], wo_ref[...])
    o_ref[...] = _layer_norm(DEEPNORM_ALPHA * h + mix, g_ref[...], b_ref[...])


def _merge_ln(h2, yr2, ya2, w_in, w_br, w_ba, w_o, ln_g, ln_b, *, layer, w_layer, gate_off, tm):
    n_tok, d_model = h2.shape
    tile = lambda: pl.BlockSpec((tm, d_model), lambda i: (i, 0))
    return pl.pallas_call(
        functools.partial(_merge_ln_kernel, gate_off=gate_off),
        grid=(n_tok // tm,),
        in_specs=[tile(), tile(), tile(),
                  *[_layer_spec(p, w_layer) for p in (w_in, w_br, w_ba, w_o)],
                  _layer_spec(ln_g, layer), _layer_spec(ln_b, layer)],
        out_specs=tile(),
        out_shape=jax.ShapeDtypeStruct((n_tok, d_model), F32),
        scratch_shapes=[pltpu.VMEM((tm, d_model), BF16)],
        compiler_params=_params(1),
        name="merge_ln",
    )(h2, yr2, ya2, w_in, w_br, w_ba, w_o, ln_g, ln_b)


def _cross_ln_kernel(h_ref, mem_ref, wq_ref, wkv_ref, wo_ref, g_ref, b_ref, *refs, n_casts):
    cast_in, o_ref, cast_out = refs[:n_casts], refs[n_casts], refs[n_casts + 1:2 * n_casts + 1]
    qk_s, vo_s, p_s = refs[2 * n_casts + 1:]
    for src, dst in zip(cast_in, cast_out):
        dst[...] = src[...].astype(BF16)
    d_model = h_ref.shape[1]
    mem_len = mem_ref.shape[0]
    head_dim = d_model // CROSS_HEADS
    scale = head_dim ** -0.5

    @pl.when(pl.program_id(1) == 0)
    def _():
        kv = _dot(mem_ref[...].astype(BF16), wkv_ref[...])
        for hd in range(CROSS_HEADS):
            cs = slice(hd * head_dim, (hd + 1) * head_dim)
            ms = slice(hd * mem_len, (hd + 1) * mem_len)
            k = (kv[:, cs] * scale).astype(BF16)
            v = kv[:, d_model + hd * head_dim:d_model + (hd + 1) * head_dim].astype(BF16)
            qk_s[:, ms] = _dot_nt(wq_ref[:, cs], k).astype(BF16)
            vo_s[ms, :] = _dot(v, wo_ref[cs, :]).astype(BF16)

    h = h_ref[...]
    s = _dot(h.astype(BF16), qk_s[...])
    for hd in range(CROSS_HEADS):
        ms = slice(hd * mem_len, (hd + 1) * mem_len)
        sh = s[:, ms]
        e = jnp.exp(sh - jnp.max(sh, axis=-1, keepdims=True))
        p_s[:, ms] = (e * (1.0 / jnp.sum(e, axis=-1, keepdims=True))).astype(BF16)
    out = _dot(p_s[...], vo_s[...])
    o_ref[...] = _layer_norm(DEEPNORM_ALPHA * h + out, g_ref[...], b_ref[...])


def _cross_ln(h, mem, w_q, w_kv, w_o, ln_g, ln_b, *, layer, w_layer, tm, cast_stacks=(),
              cast_layer=None):
    n_batch, seq, d_model = h.shape
    mem_len = mem.shape[1]
    per_batch = seq // tm
    tile = pl.BlockSpec((None, tm, d_model), lambda b, i: (b, i, 0))
    cast_in, cast_out, cast_shapes = _next_layer_cast(
        cast_stacks, cast_layer, n_batch * per_batch, lambda b, i: b * per_batch + i)
    out, *casts = pl.pallas_call(
        functools.partial(_cross_ln_kernel, n_casts=len(cast_stacks)),
        grid=(n_batch, per_batch),
        in_specs=[tile,
                  pl.BlockSpec((None, mem_len, d_model), lambda b, i: (b, 0, 0)),
                  *[_layer_spec(p, w_layer) for p in (w_q, w_kv, w_o)],
                  _layer_spec(ln_g, layer), _layer_spec(ln_b, layer), *cast_in],
        out_specs=[tile, *cast_out],
        out_shape=[jax.ShapeDtypeStruct((n_batch, seq, d_model), F32), *cast_shapes],
        scratch_shapes=[pltpu.VMEM((d_model, CROSS_HEADS * mem_len), BF16),
                        pltpu.VMEM((CROSS_HEADS * mem_len, d_model), BF16),
                        pltpu.VMEM((tm, CROSS_HEADS * mem_len), BF16)],
        compiler_params=_params(2),
        name="cross_ln",
    )(h, mem, w_q, w_kv, w_o, ln_g, ln_b, *cast_stacks)
    return out, casts


def _ffn_ln_kernel(h_ref, wi_ref, wo_ref, g_ref, b_ref, *refs, n_casts):
    cast_in, o_ref, cast_out = refs[:n_casts], refs[n_casts], refs[n_casts + 1:2 * n_casts + 1]
    act_s, = refs[2 * n_casts + 1:]
    for src, dst in zip(cast_in, cast_out):
        dst[...] = src[...].astype(BF16)
    d_ff = wo_ref.shape[0]
    h = h_ref[...]
    hb = h.astype(BF16)
    for c in range(d_ff // MXU_DIM):
        cs = slice(c * MXU_DIM, (c + 1) * MXU_DIM)
        cs_up = slice(d_ff + c * MXU_DIM, d_ff + (c + 1) * MXU_DIM)
        gate = _dot(hb, wi_ref[:, cs])
        up = _dot(hb, wi_ref[:, cs_up])
        act_s[:, cs] = (jax.nn.silu(gate) * up).astype(BF16)
    out = _dot(act_s[...], wo_ref[...])
    o_ref[...] = _layer_norm(DEEPNORM_ALPHA * h + out, g_ref[...], b_ref[...])


def _ffn_ln(h2, w_i, w_o, ln_g, ln_b, *, layer, w_layer, tm, cast_stacks=(), cast_layer=None):
    n_tok, d_model = h2.shape
    d_ff = w_o.shape[1]
    assert d_ff % MXU_DIM == 0
    n_tiles = n_tok // tm
    tile = lambda: pl.BlockSpec((tm, d_model), lambda i: (i, 0))
    cast_in, cast_out, cast_shapes = _next_layer_cast(cast_stacks, cast_layer, n_tiles, lambda i: i)
    out, *casts = pl.pallas_call(
        functools.partial(_ffn_ln_kernel, n_casts=len(cast_stacks)),
        grid=(n_tiles,),
        in_specs=[tile(), _layer_spec(w_i, w_layer), _layer_spec(w_o, w_layer),
                  _layer_spec(ln_g, layer), _layer_spec(ln_b, layer), *cast_in],
        out_specs=[tile(), *cast_out],
        out_shape=[jax.ShapeDtypeStruct((n_tok, d_model), F32), *cast_shapes],
        scratch_shapes=[pltpu.VMEM((tm, d_ff), BF16)],
        compiler_params=_params(1),
        name="ffn_ln",
    )(h2, w_i, w_o, ln_g, ln_b, *cast_stacks)
    return out, casts


def _tile_sizes(seq):
    ts = 128 if seq % 128 == 0 else seq
    tm = 1024 if seq % 1024 == 0 else seq
    return ts, tm


def kernel(x, mem, w_in, conv_w, conv_b, w_rg, b_rg, w_ig, b_ig, lru_lambda, w_br_rnn, w_br_attn, sinks, w_out, ln1_g, ln1_b, cq_w, ckv_w, co_w, ln2_g, ln2_b, ffn_wi, ffn_wo, ln3_g, ln3_b):
    n_batch, seq, d_model = x.shape
    depth = w_in.shape[0]
    n_tok = n_batch * seq
    kv_width = N_KV_HEADS * HEAD_DIM
    n_a = 3 * d_model + 2 * kv_width
    ts, tm = _tile_sizes(seq)
    rope_tabs = _rope_tables(seq)
    rows = lambda p: p.reshape(depth, 1, -1)

    w_gate = jnp.concatenate([w_rg, w_ig], axis=-1).astype(BF16)
    mixer_stacks = (w_in, w_br_rnn, w_br_attn, w_out, cq_w, ckv_w, co_w)
    ffn_stacks = (ffn_wi, ffn_wo)
    w_in_b, w_br, w_ba, w_o, w_cq, w_ckv, w_co, w_fi, w_fo = _cast_first_layer(mixer_stacks + ffn_stacks)

    h = x
    for l in range(depth):
        has_next = l + 1 < depth
        y_rnn, q, kv = _rglru_qkv(h, w_in_b, w_gate, conv_w, rows(conv_b), rows(b_rg), rows(b_ig),
                                    rows(lru_lambda), rope_tabs, layer=l, w_layer=0, ts=ts)
        y_attn = _swa(q, kv, sinks, layer=l, n_sub=8 if seq % (8 * WINDOW) == 0 else 1)
        h2 = _merge_ln(h.reshape(n_tok, d_model), y_rnn.reshape(n_tok, d_model),
                       y_attn.reshape(n_tok, d_model), w_in_b, w_br, w_ba, w_o,
                       rows(ln1_g), rows(ln1_b), layer=l, w_layer=0, gate_off=n_a, tm=tm)
        h, casts = _cross_ln(h2.reshape(n_batch, seq, d_model), mem, w_cq, w_ckv, w_co,
                             rows(ln2_g), rows(ln2_b), layer=l, w_layer=0, tm=tm,
                             cast_stacks=mixer_stacks if has_next else (), cast_layer=l + 1)
        if has_next:
            w_in_b, w_br, w_ba, w_o, w_cq, w_ckv, w_co = casts
        h2, casts = _ffn_ln(h.reshape(n_tok, d_model), w_fi, w_fo, rows(ln3_g), rows(ln3_b),
                            layer=l, w_layer=0, tm=tm,
                            cast_stacks=ffn_stacks if has_next else (), cast_layer=l + 1)
        if has_next:
            w_fi, w_fo = casts
        h = h2.reshape(n_batch, seq, d_model)
    return h
```

```python
import functools
import math

import jax
import jax.numpy as jnp
from jax import lax
from jax.experimental import pallas as pl
from jax.experimental.pallas import tpu as pltpu

F32 = jnp.float32
BF16 = jnp.bfloat16

DEPTH = 4
RNN_BLOCKS = 4
CONV_WIDTH = 4
LRU_C = 8.0
HEAD_DIM = 64
N_KV_HEADS = 2
WINDOW = 128
ROPE_THETA = 500000.0
ROT_DIM = HEAD_DIM // 4
CROSS_HEADS = 4
LN_EPS = 1e-5
DEEPNORM_ALPHA = (2 * DEPTH) ** 0.25
NEG_INF = -1e30

SUBLANES = 8
LANES = 128
MXU_DIM = 256
VMEM_LIMIT_BYTES = 56 * 1024 * 1024


def _params(n_grid_dims):
    return pltpu.CompilerParams(
        dimension_semantics=("arbitrary",) * n_grid_dims,
        vmem_limit_bytes=VMEM_LIMIT_BYTES,
    )


def _layer_spec(stack, layer):
    zeros = (0,) * (stack.ndim - 1)
    return pl.BlockSpec((None,) + stack.shape[1:], lambda *_: (layer,) + zeros,
                        pipeline_mode=pl.Buffered(1))


def _next_layer_cast(stacks, layer, n_steps, step_of):
    in_specs, out_specs, out_shapes = [], [], []
    for w in stacks:
        _, k, n = w.shape
        assert k % (n_steps * 2 * SUBLANES) == 0
        slab = k // n_steps
        in_specs.append(pl.BlockSpec((None, slab, n), lambda *g: (layer, step_of(*g), 0)))
        out_specs.append(pl.BlockSpec((None, slab, n), lambda *g: (0, step_of(*g), 0)))
        out_shapes.append(jax.ShapeDtypeStruct((1, k, n), BF16))
    return in_specs, out_specs, out_shapes


def _cast_kernel(*refs):
    n = len(refs) // 2
    for src, dst in zip(refs[:n], refs[n:]):
        dst[...] = src[...].astype(BF16)


def _cast_first_layer(stacks, n_steps=8):
    in_specs, out_specs, out_shapes = _next_layer_cast(stacks, 0, n_steps, lambda i: i)
    return pl.pallas_call(
        _cast_kernel,
        grid=(n_steps,),
        in_specs=in_specs,
        out_specs=out_specs,
        out_shape=out_shapes,
        compiler_params=_params(1),
        name="cast_first_layer",
    )(*stacks)


def _layer_norm(x, g, b):
    mu = jnp.mean(x, axis=-1, keepdims=True)
    xc = x - mu
    var = jnp.mean(xc * xc, axis=-1, keepdims=True)
    return xc * lax.rsqrt(var + LN_EPS) * g + b


def _sigmoid(x):
    return 0.5 * jnp.tanh(0.5 * x) + 0.5


def _sqrt_nonneg(x):
    return jnp.where(x > 0.0, x * lax.rsqrt(x), 0.0)


def _dot(a, b):
    return jnp.dot(a, b, preferred_element_type=F32)


def _dot_nt(a, b):
    return lax.dot_general(a, b, (((1,), (1,)), ((), ())), preferred_element_type=F32)


def _rglru_qkv_kernel(x_ref, w_ref, wgate_ref, convw_ref, convb_ref, brg_ref, big_ref, lam_ref,
                      rc_ref, rs1_ref, rs2_ref,
                      y_ref, q_ref, kv_ref,
                      tm_s, stage_s, ab_s, convc_s, hc_s, *, n_batch, ts, d_model, kv_width):
    rows = n_batch * ts
    rnn_block = d_model // RNN_BLOCKS
    tail = (CONV_WIDTH - 1) * n_batch
    pitch = _batch_pitch(ts)

    @pl.when(pl.program_id(0) == 0)
    def _():
        convc_s[...] = jnp.zeros_like(convc_s)
        hc_s[...] = jnp.zeros_like(hc_s)

    xs = jnp.concatenate([x_ref[b] for b in range(n_batch)], axis=0).astype(BF16)

    def to_time_major(val, slot):
        n_blk = val.shape[1] // LANES
        for j in range(n_blk):
            for b in range(n_batch):
                tm_s[slot * n_blk + j, b * pitch:b * pitch + ts, :] = (
                    val[b * ts:(b + 1) * ts, j * LANES:(j + 1) * LANES])
        return jnp.concatenate(
            [jnp.concatenate([tm_s[slot * n_blk + j, pl.ds(t, n_batch, stride=pitch), :]
                              for j in range(n_blk)], axis=1)
             for t in range(ts)], axis=0)

    rc, rs1, rs2 = rc_ref[...], rs1_ref[...], rs2_ref[...]
    half = ROT_DIM // 2

    def rope(tile, tabs):
        c, s1, s2 = tabs
        return (tile * c + pltpu.roll(tile, LANES - half, axis=1) * s1
                + pltpu.roll(tile, half, axis=1) * s2)

    q_off = 2 * d_model
    scale = HEAD_DIM ** -0.5
    k_tabs = (rc, rs1, rs2)
    q_tabs = (rc * scale, rs1 * scale, rs2 * scale)

    def project_q(c):
        return _dot(xs, w_ref[:, q_off + c * MXU_DIM:q_off + (c + 1) * MXU_DIM])

    def emit_q(c, qc):
        for b in range(n_batch):
            qb = qc[b * ts:(b + 1) * ts, :]
            parts = [rope(qb[:, j * LANES:(j + 1) * LANES], q_tabs) for j in range(MXU_DIM // LANES)]
            q_ref[b, :, c * MXU_DIM:(c + 1) * MXU_DIM] = jnp.concatenate(parts, axis=1).astype(BF16)

    def project_kv():
        kv_off = 3 * d_model
        return _dot(xs, w_ref[:, kv_off:kv_off + 2 * kv_width])

    def emit_kv(kv):
        for b in range(n_batch):
            kvb = kv[b * ts:(b + 1) * ts, :]
            kb = rope(kvb[:, :kv_width], k_tabs)
            vb = kvb[:, kv_width:]
            kv_ref[b] = jnp.concatenate(
                [kb, pltpu.roll(kb, HEAD_DIM, axis=1), vb, pltpu.roll(vb, HEAD_DIM, axis=1)],
                axis=1).astype(BF16)

    neg_c_softplus = -LRU_C * jax.nn.softplus(-lam_ref[...])

    for n in range(RNN_BLOCKS):
        cs = slice(n * rnn_block, (n + 1) * rnn_block)
        xr_rows = _dot(xs, w_ref[:, cs])
        qc = project_q(n)
        xr = to_time_major(xr_rows, n)
        hist = convc_s[:, cs]
        ext = jnp.concatenate([hist, xr], axis=0)
        convc_s[:, cs] = xr[rows - tail:, :]
        xc = convb_ref[:, cs]
        for kk in range(CONV_WIDTH):
            xc = xc + ext[kk * n_batch:kk * n_batch + rows, :] * convw_ref[kk:kk + 1, cs]
        gates = _dot(xc.astype(BF16), wgate_ref[n])
        emit_q(n, qc)
        r = _sigmoid(gates[:, :rnn_block] + brg_ref[:, cs])
        i = _sigmoid(gates[:, rnn_block:] + big_ref[:, cs])
        log_a = neg_c_softplus[:, cs] * r
        a = jnp.exp(log_a)
        mult = _sqrt_nonneg(-jnp.tanh(log_a) * (a * a + 1.0))
        ab_s[0, :, cs] = a
        ab_s[1, :, cs] = mult * (i * xc)

    h = hc_s[...]
    for t in range(ts):
        rs = slice(t * n_batch, (t + 1) * n_batch)
        h = ab_s[0, rs, :] * h + ab_s[1, rs, :]
        for j in range(d_model // LANES):
            stage_s[j, rs, :] = h[:, j * LANES:(j + 1) * LANES]
    hc_s[...] = h

    blk_lanes = rnn_block // LANES
    for n in range(RNN_BLOCKS):
        gate = jax.nn.gelu(
            _dot(xs, w_ref[:, d_model + n * rnn_block:d_model + (n + 1) * rnn_block]))
        for b in range(n_batch):
            hb = jnp.concatenate(
                [stage_s[n * blk_lanes + j, pl.ds(b, ts, stride=n_batch), :] for j in range(blk_lanes)],
                axis=1)
            y_ref[b, :, n * rnn_block:(n + 1) * rnn_block] = (
                hb * gate[b * ts:(b + 1) * ts, :]).astype(BF16)
    emit_kv(project_kv())


def _batch_pitch(ts):
    groups = ts // SUBLANES
    return SUBLANES * (groups + 1 if groups % 2 == 0 else groups)


def _rope_tables(seq_len):
    half = ROT_DIM // 2
    pos = jnp.arange(seq_len, dtype=F32)
    inv_freq = ROPE_THETA ** (-jnp.arange(0, ROT_DIM, 2, dtype=F32) / ROT_DIM)
    ang = pos[:, None] * inv_freq[None, :]
    cos, sin = jnp.cos(ang), jnp.sin(ang)
    ones = jnp.ones((seq_len, HEAD_DIM - ROT_DIM), F32)
    zeros = jnp.zeros((seq_len, HEAD_DIM - ROT_DIM), F32)
    zh = jnp.zeros((seq_len, half), F32)
    c_head = jnp.concatenate([cos, cos, ones], axis=1)
    s1_head = jnp.concatenate([-sin, zh, zeros], axis=1)
    s2_head = jnp.concatenate([zh, sin, zeros], axis=1)
    reps = LANES // HEAD_DIM
    return (jnp.tile(c_head, (1, reps)), jnp.tile(s1_head, (1, reps)), jnp.tile(s2_head, (1, reps)))


def _rglru_qkv(h, w_a, w_gate, conv_w, conv_b, b_rg, b_ig, lam, rope_tabs, *, layer, w_layer, ts):
    n_batch, seq, d_model = h.shape
    kv_width = N_KV_HEADS * HEAD_DIM
    assert n_batch == SUBLANES and seq % ts == 0 and ts % SUBLANES == 0 and kv_width == LANES
    rows = n_batch * ts
    kernel = functools.partial(_rglru_qkv_kernel, n_batch=n_batch, ts=ts, d_model=d_model,
                               kv_width=kv_width)
    tile3 = lambda width: pl.BlockSpec((n_batch, ts, width), lambda i: (0, i, 0))
    rope_spec = pl.BlockSpec((ts, LANES), lambda i: (i, 0))
    return pl.pallas_call(
        kernel,
        grid=(seq // ts,),
        in_specs=[
            tile3(d_model),
            _layer_spec(w_a, w_layer),
            *[_layer_spec(p, layer) for p in (w_gate, conv_w, conv_b, b_rg, b_ig, lam)],
            rope_spec, rope_spec, rope_spec,
        ],
        out_specs=[tile3(d_model), tile3(d_model), tile3(4 * kv_width)],
        out_shape=[
            jax.ShapeDtypeStruct((n_batch, seq, d_model), BF16),
            jax.ShapeDtypeStruct((n_batch, seq, d_model), BF16),
            jax.ShapeDtypeStruct((n_batch, seq, 4 * kv_width), BF16),
        ],
        scratch_shapes=[
            pltpu.VMEM((d_model // LANES, n_batch * _batch_pitch(ts), LANES), F32),
            pltpu.VMEM((d_model // LANES, rows, LANES), F32),
            pltpu.VMEM((2, rows, d_model), F32),
            pltpu.VMEM(((CONV_WIDTH - 1) * n_batch, d_model), F32),
            pltpu.VMEM((n_batch, d_model), F32),
        ],
        compiler_params=_params(1),
        name="rglru_qkv",
    )(h, w_a, w_gate, conv_w, conv_b, b_rg, b_ig, lam, *rope_tabs)


def _swa_kernel(sinks_ref, q_ref, kvp_ref, kvc_ref, o_ref, *, layer, n_q_heads, blk):
    n_sub = q_ref.shape[0] // blk
    group = n_q_heads // N_KV_HEADS
    pairs = group // 2
    row = lax.broadcasted_iota(jnp.int32, (blk, blk), 0)
    col = lax.broadcasted_iota(jnp.int32, (blk, blk), 1)
    upper = col > row
    no_prev = jnp.where(pl.program_id(1) == 0, NEG_INF, 0.0)
    first_bias = jnp.where(upper, no_prev, 0.0)
    low_lanes = lax.broadcasted_iota(jnp.int32, (blk, LANES), 1) < HEAD_DIM
    low2 = lax.broadcasted_iota(jnp.int32, (2 * blk, LANES), 1) < HEAD_DIM
    head_ind = jnp.concatenate([jnp.where(low2, 1.0, 0.0), jnp.where(low2, 0.0, 1.0)],
                               axis=0).astype(BF16)

    def keep(x, low):
        return jnp.where(low_lanes if low else ~low_lanes, x, jnp.zeros_like(x))

    for sb in range(n_sub):
        rows = slice(sb * blk, (sb + 1) * blk)

        def prev(j):
            if sb == 0:
                return kvp_ref[:, j * LANES:(j + 1) * LANES]
            return kvc_ref[(sb - 1) * blk:sb * blk, j * LANES:(j + 1) * LANES]

        def cur(j):
            return kvc_ref[rows, j * LANES:(j + 1) * LANES]

        for g in range(N_KV_HEADS):
            low_src, high_src = (0, 1) if g == 0 else (1, 0)
            kk = jnp.concatenate([keep(prev(low_src), True), keep(cur(low_src), True),
                                  keep(prev(high_src), False), keep(cur(high_src), False)], axis=0)
            vv = jnp.concatenate([keep(prev(2 + low_src), True), keep(cur(2 + low_src), True),
                                  keep(prev(2 + high_src), False), keep(cur(2 + high_src), False)], axis=0)
            qg = jnp.concatenate(
                [q_ref[rows, (g * pairs + p) * LANES:(g * pairs + p + 1) * LANES] for p in range(pairs)], axis=0)
            s = _dot_nt(qg, kk)
            probs, sink_logits = [], []
            for p in range(pairs):
                sp = s[p * blk:(p + 1) * blk, :]
                row_probs, rel = [], []
                for half in range(2):
                    sink = sinks_ref[layer, g * group + 2 * p + half]
                    comb = jnp.where(upper, sp[:, 2 * half * blk:(2 * half + 1) * blk],
                                     sp[:, (2 * half + 1) * blk:(2 * half + 2) * blk])
                    if sb == 0:
                        comb = comb + first_bias
                    m = jnp.maximum(jnp.max(comb, axis=-1, keepdims=True), sink)
                    e = jnp.exp(comb - m)
                    rel.append(sink - m)
                    row_probs += [jnp.where(upper, e, 0.0).astype(BF16), jnp.where(upper, 0.0, e).astype(BF16)]
                probs.append(jnp.concatenate(row_probs, axis=1))
                sink_logits.append(jnp.where(low_lanes, rel[0], rel[1]))
            o = _dot(jnp.concatenate(probs, axis=0), jnp.concatenate([vv, head_ind], axis=1))
            for p in range(pairs):
                prs = slice(p * blk, (p + 1) * blk)
                denom = o[prs, LANES:] + jnp.exp(sink_logits[p])
                j = g * pairs + p
                o_ref[rows, j * LANES:(j + 1) * LANES] = (o[prs, :LANES] * (1.0 / denom)).astype(o_ref.dtype)


def _swa(q, kv, sinks, *, layer, n_sub):
    n_batch, seq, width = q.shape
    blk = WINDOW
    n_q_heads = width // HEAD_DIM
    assert 2 * HEAD_DIM == LANES and N_KV_HEADS == 2 and kv.shape[-1] == 4 * LANES
    assert seq % (n_sub * blk) == 0
    prev = lambda b, n: (b, jnp.maximum(n_sub * n - 1, 0), 0)
    cur = lambda b, n: (b, n, 0)
    return pl.pallas_call(
        functools.partial(_swa_kernel, layer=layer, n_q_heads=n_q_heads, blk=blk),
        grid=(n_batch, seq // (n_sub * blk)),
        in_specs=[
            pl.BlockSpec(memory_space=pltpu.SMEM),
            pl.BlockSpec((None, n_sub * blk, width), cur),
            pl.BlockSpec((None, blk, kv.shape[-1]), prev),
            pl.BlockSpec((None, n_sub * blk, kv.shape[-1]), cur),
        ],
        out_specs=pl.BlockSpec((None, n_sub * blk, width), cur),
        out_shape=jax.ShapeDtypeStruct((n_batch, seq, width), BF16),
        compiler_params=_params(2),
        name="swa",
    )(sinks, q, kv, kv)


def _merge_ln_kernel(h_ref, yr_ref, ya_ref, win_ref, wbr_ref, wba_ref, wo_ref, g_ref, b_ref, o_ref,
                     merged_s, *, gate_off):
    d_model = h_ref.shape[1]
    h = h_ref[...]
    hb = h.astype(BF16)
    yr = yr_ref[...]
    ya = ya_ref[...]
    for c in range(d_model // MXU_DIM):
        cs = slice(c * MXU_DIM, (c + 1) * MXU_DIM)
        gs_r = slice(gate_off + c * MXU_DIM, gate_off + (c + 1) * MXU_DIM)
        gs_a = slice(gate_off + d_model + c * MXU_DIM, gate_off + d_model + (c + 1) * MXU_DIM)
        gate_r = _sigmoid(_dot(hb, win_ref[:, gs_r]))
        gate_a = _sigmoid(_dot(hb, win_ref[:, gs_a]))
        merged = gate_r * _dot(yr, wbr_ref[:, cs]) + gate_a * _dot(ya, wba_ref[:, cs])
        merged_s[:, cs] = merged.astype(BF16)
    mix = _dot(merged_s[...], wo_ref[...])
    o_ref[...] = _layer_norm(DEEPNORM_ALPHA * h + mix, g_ref[...], b_ref[...])


def _merge_ln(h2, yr2, ya2, w_in, w_br, w_ba, w_o, ln_g, ln_b, *, layer, w_layer, gate_off, tm):
    n_tok, d_model = h2.shape
    tile = lambda: pl.BlockSpec((tm, d_model), lambda i: (i, 0))
    return pl.pallas_call(
        functools.partial(_merge_ln_kernel, gate_off=gate_off),
        grid=(n_tok // tm,),
        in_specs=[tile(), tile(), tile(),
                  *[_layer_spec(p, w_layer) for p in (w_in, w_br, w_ba, w_o)],
                  _layer_spec(ln_g, layer), _layer_spec(ln_b, layer)],
        out_specs=tile(),
        out_shape=jax.ShapeDtypeStruct((n_tok, d_model), F32),
        scratch_shapes=[pltpu.VMEM((tm, d_model), BF16)],
        compiler_params=_params(1),
        name="merge_ln",
    )(h2, yr2, ya2, w_in, w_br, w_ba, w_o, ln_g, ln_b)


def _cross_ln_kernel(h_ref, mem_ref, wq_ref, wkv_ref, wo_ref, g_ref, b_ref, *refs, n_casts):
    cast_in, o_ref, cast_out = refs[:n_casts], refs[n_casts], refs[n_casts + 1:2 * n_casts + 1]
    qk_s, vo_s, p_s = refs[2 * n_casts + 1:]
    for src, dst in zip(cast_in, cast_out):
        dst[...] = src[...].astype(BF16)
    d_model = h_ref.shape[1]
    mem_len = mem_ref.shape[0]
    head_dim = d_model // CROSS_HEADS
    scale = head_dim ** -0.5

    @pl.when(pl.program_id(1) == 0)
    def _():
        kv = _dot(mem_ref[...].astype(BF16), wkv_ref[...])
        for hd in range(CROSS_HEADS):
            cs = slice(hd * head_dim, (hd + 1) * head_dim)
            ms = slice(hd * mem_len, (hd + 1) * mem_len)
            k = (kv[:, cs] * scale).astype(BF16)
            v = kv[:, d_model + hd * head_dim:d_model + (hd + 1) * head_dim].astype(BF16)
            qk_s[:, ms] = _dot_nt(wq_ref[:, cs], k).astype(BF16)
            vo_s[ms, :] = _dot(v, wo_ref[cs, :]).astype(BF16)

    h = h_ref[...]
    s = _dot(h.astype(BF16), qk_s[...])
    for hd in range(CROSS_HEADS):
        ms = slice(hd * mem_len, (hd + 1) * mem_len)
        sh = s[:, ms]
        e = jnp.exp(sh - jnp.max(sh, axis=-1, keepdims=True))
        p_s[:, ms] = (e * (1.0 / jnp.sum(e, axis=-1, keepdims=True))).astype(BF16)
    out = _dot(p_s[...], vo_s[...])
    o_ref[...] = _layer_norm(DEEPNORM_ALPHA * h + out, g_ref[...], b_ref[...])


def _cross_ln(h, mem, w_q, w_kv, w_o, ln_g, ln_b, *, layer, w_layer, tm, cast_stacks=(),
              cast_layer=None):
    n_batch, seq, d_model = h.shape
    mem_len = mem.shape[1]
    per_batch = seq // tm
    tile = pl.BlockSpec((None, tm, d_model), lambda b, i: (b, i, 0))
    cast_in, cast_out, cast_shapes = _next_layer_cast(
        cast_stacks, cast_layer, n_batch * per_batch, lambda b, i: b * per_batch + i)
    out, *casts = pl.pallas_call(
        functools.partial(_cross_ln_kernel, n_casts=len(cast_stacks)),
        grid=(n_batch, per_batch),
        in_specs=[tile,
                  pl.BlockSpec((None, mem_len, d_model), lambda b, i: (b, 0, 0)),
                  *[_layer_spec(p, w_layer) for p in (w_q, w_kv, w_o)],
                  _layer_spec(ln_g, layer), _layer_spec(ln_b, layer), *cast_in],
        out_specs=[tile, *cast_out],
        out_shape=[jax.ShapeDtypeStruct((n_batch, seq, d_model), F32), *cast_shapes],
        scratch_shapes=[pltpu.VMEM((d_model, CROSS_HEADS * mem_len), BF16),
                        pltpu.VMEM((CROSS_HEADS * mem_len, d_model), BF16),
                        pltpu.VMEM((tm, CROSS_HEADS * mem_len), BF16)],
        compiler_params=_params(2),
        name="cross_ln",
    )(h, mem, w_q, w_kv, w_o, ln_g, ln_b, *cast_stacks)
    return out, casts


def _ffn_ln_kernel(h_ref, wi_ref, wo_ref, g_ref, b_ref, *refs, n_casts):
    cast_in, o_ref, cast_out = refs[:n_casts], refs[n_casts], refs[n_casts + 1:2 * n_casts + 1]
    act_s, = refs[2 * n_casts + 1:]
    for src, dst in zip(cast_in, cast_out):
        dst[...] = src[...].astype(BF16)
    d_ff = wo_ref.shape[0]
    h = h_ref[...]
    hb = h.astype(BF16)
    for c in range(d_ff // MXU_DIM):
        cs = slice(c * MXU_DIM, (c + 1) * MXU_DIM)
        cs_up = slice(d_ff + c * MXU_DIM, d_ff + (c + 1) * MXU_DIM)
        gate = _dot(hb, wi_ref[:, cs])
        up = _dot(hb, wi_ref[:, cs_up])
        act_s[:, cs] = (jax.nn.silu(gate) * up).astype(BF16)
    out = _dot(act_s[...], wo_ref[...])
    o_ref[...] = _layer_norm(DEEPNORM_ALPHA * h + out, g_ref[...], b_ref[...])


def _ffn_ln(h2, w_i, w_o, ln_g, ln_b, *, layer, w_layer, tm, cast_stacks=(), cast_layer=None):
    n_tok, d_model = h2.shape
    d_ff = w_o.shape[1]
    assert d_ff % MXU_DIM == 0
    n_tiles = n_tok // tm
    tile = lambda: pl.BlockSpec((tm, d_model), lambda i: (i, 0))
    cast_in, cast_out, cast_shapes = _next_layer_cast(cast_stacks, cast_layer, n_tiles, lambda i: i)
    out, *casts = pl.pallas_call(
        functools.partial(_ffn_ln_kernel, n_casts=len(cast_stacks)),
        grid=(n_tiles,),
        in_specs=[tile(), _layer_spec(w_i, w_layer), _layer_spec(w_o, w_layer),
                  _layer_spec(ln_g, layer), _layer_spec(ln_b, layer), *cast_in],
        out_specs=[tile(), *cast_out],
        out_shape=[jax.ShapeDtypeStruct((n_tok, d_model), F32), *cast_shapes],
        scratch_shapes=[pltpu.VMEM((tm, d_ff), BF16)],
        compiler_params=_params(1),
        name="ffn_ln",
    )(h2, w_i, w_o, ln_g, ln_b, *cast_stacks)
    return out, casts


def _tile_sizes(seq):
    ts = 128 if seq % 128 == 0 else seq
    tm = 1024 if seq % 1024 == 0 else seq
    return ts, tm


def kernel(x, mem, w_in, conv_w, conv_b, w_rg, b_rg, w_ig, b_ig, lru_lambda, w_br_rnn, w_br_attn, sinks, w_out, ln1_g, ln1_b, cq_w, ckv_w, co_w, ln2_g, ln2_b, ffn_wi, ffn_wo, ln3_g, ln3_b):
    n_batch, seq, d_model = x.shape
    depth = w_in.shape[0]
    n_tok = n_batch * seq
    kv_width = N_KV_HEADS * HEAD_DIM
    n_a = 3 * d_model + 2 * kv_width
    ts, tm = _tile_sizes(seq)
    rope_tabs = _rope_tables(seq)
    rows = lambda p: p.reshape(depth, 1, -1)

    w_gate = jnp.concatenate([w_rg, w_ig], axis=-1).astype(BF16)
    mixer_stacks = (w_in, w_br_rnn, w_br_attn, w_out, cq_w, ckv_w, co_w)
    ffn_stacks = (ffn_wi, ffn_wo)
    w_in_b, w_br, w_ba, w_o, w_cq, w_ckv, w_co, w_fi, w_fo = _cast_first_layer(mixer_stacks + ffn_stacks)

    h = x
    for l in range(depth):
        has_next = l + 1 < depth
        y_rnn, q, kv = _rglru_qkv(h, w_in_b, w_gate, conv_w, rows(conv_b), rows(b_rg), rows(b_ig),
                                    rows(lru_lambda), rope_tabs, layer=l, w_layer=0, ts=ts)
        y_attn = _swa(q, kv, sinks, layer=l, n_sub=8 if seq % (8 * WINDOW) == 0 else 1)
        h2 = _merge_ln(h.reshape(n_tok, d_model), y_rnn.reshape(n_tok, d_model),
                       y_attn.reshape(n_tok, d_model), w_in_b, w_br, w_ba, w_o,
                       rows(ln1_g), rows(ln1_b), layer=l, w_layer=0, gate_off=n_a, tm=tm)
        h, casts = _cross_ln(h2.reshape(n_batch, seq, d_model), mem, w_cq, w_ckv, w_co,
                             rows(ln2_g), rows(ln2_b), layer=l, w_layer=0, tm=tm,
                             cast_stacks=mixer_stacks if has_next else (), cast_layer=l + 1)
        if has_next:
            w_in_b, w_br, w_ba, w_o, w_cq, w_ckv, w_co = casts
        h2, casts = _ffn_ln(h.reshape(n_tok, d_model), w_fi, w_fo, rows(ln3_g), rows(ln3_b),
                            layer=l, w_layer=0, tm=tm,
                            cast_stacks=ffn_stacks if has_next else (), cast_layer=l + 1)
        if has_next:
            w_fi, w_fo = casts
        h = h2.reshape(n_batch, seq, d_model)
    return h
```

```python
import functools
import math

import jax
import jax.numpy as jnp
from jax import lax
from jax.experimental import pallas as pl
from jax.experimental.pallas import tpu as pltpu

F32 = jnp.float32
BF16 = jnp.bfloat16

DEPTH = 4
RNN_BLOCKS = 4
CONV_WIDTH = 4
LRU_C = 8.0
HEAD_DIM = 64
N_KV_HEADS = 2
WINDOW = 128
ROPE_THETA = 500000.0
ROT_DIM = HEAD_DIM // 4
CROSS_HEADS = 4
LN_EPS = 1e-5
DEEPNORM_ALPHA = (2 * DEPTH) ** 0.25
NEG_INF = -1e30

SUBLANES = 8
LANES = 128
MXU_DIM = 256
VMEM_LIMIT_BYTES = 56 * 1024 * 1024


def _params(n_grid_dims):
    return pltpu.CompilerParams(
        dimension_semantics=("arbitrary",) * n_grid_dims,
        vmem_limit_bytes=VMEM_LIMIT_BYTES,
    )


def _layer_spec(stack, layer):
    zeros = (0,) * (stack.ndim - 1)
    return pl.BlockSpec((None,) + stack.shape[1:], lambda *_: (layer,) + zeros,
                        pipeline_mode=pl.Buffered(1))


def _next_layer_cast(stacks, layer, n_steps, step_of):
    in_specs, out_specs, out_shapes = [], [], []
    for w in stacks:
        _, k, n = w.shape
        assert k % (n_steps * 2 * SUBLANES) == 0
        slab = k // n_steps
        in_specs.append(pl.BlockSpec((None, slab, n), lambda *g: (layer, step_of(*g), 0)))
        out_specs.append(pl.BlockSpec((None, slab, n), lambda *g: (0, step_of(*g), 0)))
        out_shapes.append(jax.ShapeDtypeStruct((1, k, n), BF16))
    return in_specs, out_specs, out_shapes


def _cast_kernel(*refs):
    n = len(refs) // 2
    for src, dst in zip(refs[:n], refs[n:]):
        dst[...] = src[...].astype(BF16)


def _cast_first_layer(stacks, n_steps=8):
    in_specs, out_specs, out_shapes = _next_layer_cast(stacks, 0, n_steps, lambda i: i)
    return pl.pallas_call(
        _cast_kernel,
        grid=(n_steps,),
        in_specs=in_specs,
        out_specs=out_specs,
        out_shape=out_shapes,
        compiler_params=_params(1),
        name="cast_first_layer",
    )(*stacks)


def _layer_norm(x, g, b):
    mu = jnp.mean(x, axis=-1, keepdims=True)
    xc = x - mu
    var = jnp.mean(xc * xc, axis=-1, keepdims=True)
    return xc * lax.rsqrt(var + LN_EPS) * g + b


def _project_residual_norm(h_ref, x_s, w_ref, g_ref, b_ref, o_ref, n_split=4):
    rows = h_ref.shape[0] // n_split
    blocks = [slice(j * rows, (j + 1) * rows) for j in range(n_split)]
    outs = [_dot(x_s[rs, :], w_ref[...]) for rs in blocks]
    for rs, out in zip(blocks, outs):
        o_ref[rs, :] = _layer_norm(DEEPNORM_ALPHA * h_ref[rs, :] + out, g_ref[...], b_ref[...])


def _sigmoid(x):
    return 0.5 * jnp.tanh(0.5 * x) + 0.5


def _sqrt_nonneg(x):
    return jnp.where(x > 0.0, x * lax.rsqrt(x), 0.0)


def _dot(a, b):
    return jnp.dot(a, b, preferred_element_type=F32)


def _dot_nt(a, b):
    return lax.dot_general(a, b, (((1,), (1,)), ((), ())), preferred_element_type=F32)


def _rglru_qkv_kernel(x_ref, w_ref, wgate_ref, convw_ref, convb_ref, brg_ref, big_ref, lam_ref,
                      rc_ref, rs1_ref, rs2_ref,
                      y_ref, q_ref, kv_ref,
                      tm_s, stage_s, ab_s, convc_s, hc_s, *, n_batch, ts, d_model, kv_width):
    rows = n_batch * ts
    rnn_block = d_model // RNN_BLOCKS
    tail = (CONV_WIDTH - 1) * n_batch
    pitch = _batch_pitch(ts)

    @pl.when(pl.program_id(0) == 0)
    def _():
        convc_s[...] = jnp.zeros_like(convc_s)
        hc_s[...] = jnp.zeros_like(hc_s)

    xs = jnp.concatenate([x_ref[b] for b in range(n_batch)], axis=0).astype(BF16)

    def to_time_major(val, slot):
        n_blk = val.shape[1] // LANES
        for j in range(n_blk):
            for b in range(n_batch):
                tm_s[slot * n_blk + j, b * pitch:b * pitch + ts, :] = (
                    val[b * ts:(b + 1) * ts, j * LANES:(j + 1) * LANES])
        return jnp.concatenate(
            [jnp.concatenate([tm_s[slot * n_blk + j, pl.ds(t, n_batch, stride=pitch), :]
                              for j in range(n_blk)], axis=1)
             for t in range(ts)], axis=0)

    rc, rs1, rs2 = rc_ref[...], rs1_ref[...], rs2_ref[...]
    half = ROT_DIM // 2

    def rope(tile, tabs):
        c, s1, s2 = tabs
        return (tile * c + pltpu.roll(tile, LANES - half, axis=1) * s1
                + pltpu.roll(tile, half, axis=1) * s2)

    q_off = 2 * d_model
    scale = HEAD_DIM ** -0.5
    k_tabs = (rc, rs1, rs2)
    q_tabs = (rc * scale, rs1 * scale, rs2 * scale)

    def project_q(c):
        return _dot(xs, w_ref[:, q_off + c * MXU_DIM:q_off + (c + 1) * MXU_DIM])

    def emit_q(c, qc):
        for b in range(n_batch):
            qb = qc[b * ts:(b + 1) * ts, :]
            parts = [rope(qb[:, j * LANES:(j + 1) * LANES], q_tabs) for j in range(MXU_DIM // LANES)]
            q_ref[b, :, c * MXU_DIM:(c + 1) * MXU_DIM] = jnp.concatenate(parts, axis=1).astype(BF16)

    def project_kv():
        kv_off = 3 * d_model
        return _dot(xs, w_ref[:, kv_off:kv_off + 2 * kv_width])

    def emit_kv(kv):
        for b in range(n_batch):
            kvb = kv[b * ts:(b + 1) * ts, :]
            kb = rope(kvb[:, :kv_width], k_tabs)
            vb = kvb[:, kv_width:]
            kv_ref[b] = jnp.concatenate(
                [kb, pltpu.roll(kb, HEAD_DIM, axis=1), vb, pltpu.roll(vb, HEAD_DIM, axis=1)],
                axis=1).astype(BF16)

    neg_c_softplus = -LRU_C * jax.nn.softplus(-lam_ref[...])

    for n in range(RNN_BLOCKS):
        cs = slice(n * rnn_block, (n + 1) * rnn_block)
        xr_rows = _dot(xs, w_ref[:, cs])
        qc = project_q(n)
        xr = to_time_major(xr_rows, n)
        hist = convc_s[:, cs]
        ext = jnp.concatenate([hist, xr], axis=0)
        convc_s[:, cs] = xr[rows - tail:, :]
        xc = convb_ref[:, cs]
        for kk in range(CONV_WIDTH):
            xc = xc + ext[kk * n_batch:kk * n_batch + rows, :] * convw_ref[kk:kk + 1, cs]
        gates = _dot(xc.astype(BF16), wgate_ref[n])
        emit_q(n, qc)
        r = _sigmoid(gates[:, :rnn_block] + brg_ref[:, cs])
        i = _sigmoid(gates[:, rnn_block:] + big_ref[:, cs])
        log_a = neg_c_softplus[:, cs] * r
        a = jnp.exp(log_a)
        mult = _sqrt_nonneg(-jnp.tanh(log_a) * (a * a + 1.0))
        ab_s[0, :, cs] = a
        ab_s[1, :, cs] = mult * (i * xc)

    h = hc_s[...]
    for t in range(ts):
        rs = slice(t * n_batch, (t + 1) * n_batch)
        h = ab_s[0, rs, :] * h + ab_s[1, rs, :]
        for j in range(d_model // LANES):
            stage_s[j, rs, :] = h[:, j * LANES:(j + 1) * LANES]
    hc_s[...] = h

    blk_lanes = rnn_block // LANES
    for n in range(RNN_BLOCKS):
        gate = jax.nn.gelu(
            _dot(xs, w_ref[:, d_model + n * rnn_block:d_model + (n + 1) * rnn_block]))
        for b in range(n_batch):
            hb = jnp.concatenate(
                [stage_s[n * blk_lanes + j, pl.ds(b, ts, stride=n_batch), :] for j in range(blk_lanes)],
                axis=1)
            y_ref[b, :, n * rnn_block:(n + 1) * rnn_block] = (
                hb * gate[b * ts:(b + 1) * ts, :]).astype(BF16)
    emit_kv(project_kv())


def _batch_pitch(ts):
    groups = ts // SUBLANES
    return SUBLANES * (groups + 1 if groups % 2 == 0 else groups)


def _rope_tables(seq_len):
    half = ROT_DIM // 2
    pos = jnp.arange(seq_len, dtype=F32)
    inv_freq = ROPE_THETA ** (-jnp.arange(0, ROT_DIM, 2, dtype=F32) / ROT_DIM)
    ang = pos[:, None] * inv_freq[None, :]
    cos, sin = jnp.cos(ang), jnp.sin(ang)
    ones = jnp.ones((seq_len, HEAD_DIM - ROT_DIM), F32)
    zeros = jnp.zeros((seq_len, HEAD_DIM - ROT_DIM), F32)
    zh = jnp.zeros((seq_len, half), F32)
    c_head = jnp.concatenate([cos, cos, ones], axis=1)
    s1_head = jnp.concatenate([-sin, zh, zeros], axis=1)
    s2_head = jnp.concatenate([zh, sin, zeros], axis=1)
    reps = LANES // HEAD_DIM
    return (jnp.tile(c_head, (1, reps)), jnp.tile(s1_head, (1, reps)), jnp.tile(s2_head, (1, reps)))


def _rglru_qkv(h, w_a, w_gate, conv_w, conv_b, b_rg, b_ig, lam, rope_tabs, *, layer, w_layer, ts):
    n_batch, seq, d_model = h.shape
    kv_width = N_KV_HEADS * HEAD_DIM
    assert n_batch == SUBLANES and seq % ts == 0 and ts % SUBLANES == 0 and kv_width == LANES
    rows = n_batch * ts
    kernel = functools.partial(_rglru_qkv_kernel, n_batch=n_batch, ts=ts, d_model=d_model,
                               kv_width=kv_width)
    tile3 = lambda width: pl.BlockSpec((n_batch, ts, width), lambda i: (0, i, 0))
    rope_spec = pl.BlockSpec((ts, LANES), lambda i: (i, 0))
    return pl.pallas_call(
        kernel,
        grid=(seq // ts,),
        in_specs=[
            tile3(d_model),
            _layer_spec(w_a, w_layer),
            *[_layer_spec(p, layer) for p in (w_gate, conv_w, conv_b, b_rg, b_ig, lam)],
            rope_spec, rope_spec, rope_spec,
        ],
        out_specs=[tile3(d_model), tile3(d_model), tile3(4 * kv_width)],
        out_shape=[
            jax.ShapeDtypeStruct((n_batch, seq, d_model), BF16),
            jax.ShapeDtypeStruct((n_batch, seq, d_model), BF16),
            jax.ShapeDtypeStruct((n_batch, seq, 4 * kv_width), BF16),
        ],
        scratch_shapes=[
            pltpu.VMEM((d_model // LANES, n_batch * _batch_pitch(ts), LANES), F32),
            pltpu.VMEM((d_model // LANES, rows, LANES), F32),
            pltpu.VMEM((2, rows, d_model), F32),
            pltpu.VMEM(((CONV_WIDTH - 1) * n_batch, d_model), F32),
            pltpu.VMEM((n_batch, d_model), F32),
        ],
        compiler_params=_params(1),
        name="rglru_qkv",
    )(h, w_a, w_gate, conv_w, conv_b, b_rg, b_ig, lam, *rope_tabs)


def _swa_kernel(sinks_ref, q_ref, kvp_ref, kvc_ref, o_ref, *, layer, n_q_heads, blk):
    n_sub = q_ref.shape[0] // blk
    group = n_q_heads // N_KV_HEADS
    pairs = group // 2
    row = lax.broadcasted_iota(jnp.int32, (blk, blk), 0)
    col = lax.broadcasted_iota(jnp.int32, (blk, blk), 1)
    upper = col > row
    no_prev = jnp.where(pl.program_id(1) == 0, NEG_INF, 0.0)
    first_bias = jnp.where(upper, no_prev, 0.0)
    low_lanes = lax.broadcasted_iota(jnp.int32, (blk, LANES), 1) < HEAD_DIM
    low2 = lax.broadcasted_iota(jnp.int32, (2 * blk, LANES), 1) < HEAD_DIM
    head_ind = jnp.concatenate([jnp.where(low2, 1.0, 0.0), jnp.where(low2, 0.0, 1.0)],
                               axis=0).astype(BF16)

    def keep(x, low):
        return jnp.where(low_lanes if low else ~low_lanes, x, jnp.zeros_like(x))

    for sb in range(n_sub):
        rows = slice(sb * blk, (sb + 1) * blk)

        def prev(j):
            if sb == 0:
                return kvp_ref[:, j * LANES:(j + 1) * LANES]
            return kvc_ref[(sb - 1) * blk:sb * blk, j * LANES:(j + 1) * LANES]

        def cur(j):
            return kvc_ref[rows, j * LANES:(j + 1) * LANES]

        for g in range(N_KV_HEADS):
            low_src, high_src = (0, 1) if g == 0 else (1, 0)
            kk = jnp.concatenate([keep(prev(low_src), True), keep(cur(low_src), True),
                                  keep(prev(high_src), False), keep(cur(high_src), False)], axis=0)
            vv = jnp.concatenate([keep(prev(2 + low_src), True), keep(cur(2 + low_src), True),
                                  keep(prev(2 + high_src), False), keep(cur(2 + high_src), False)], axis=0)
            qg = jnp.concatenate(
                [q_ref[rows, (g * pairs + p) * LANES:(g * pairs + p + 1) * LANES] for p in range(pairs)], axis=0)
            s = _dot_nt(qg, kk)
            probs, sink_logits = [], []
            for p in range(pairs):
                sp = s[p * blk:(p + 1) * blk, :]
                row_probs, rel = [], []
                for half in range(2):
                    sink = sinks_ref[layer, g * group + 2 * p + half]
                    comb = jnp.where(upper, sp[:, 2 * half * blk:(2 * half + 1) * blk],
                                     sp[:, (2 * half + 1) * blk:(2 * half + 2) * blk])
                    if sb == 0:
                        comb = comb + first_bias
                    m = jnp.maximum(jnp.max(comb, axis=-1, keepdims=True), sink)
                    e = jnp.exp(comb - m)
                    rel.append(sink - m)
                    row_probs += [jnp.where(upper, e, 0.0).astype(BF16), jnp.where(upper, 0.0, e).astype(BF16)]
                probs.append(jnp.concatenate(row_probs, axis=1))
                sink_logits.append(jnp.where(low_lanes, rel[0], rel[1]))
            o = _dot(jnp.concatenate(probs, axis=0), jnp.concatenate([vv, head_ind], axis=1))
            for p in range(pairs):
                prs = slice(p * blk, (p + 1) * blk)
                denom = o[prs, LANES:] + jnp.exp(sink_logits[p])
                j = g * pairs + p
                o_ref[rows, j * LANES:(j + 1) * LANES] = (o[prs, :LANES] * (1.0 / denom)).astype(o_ref.dtype)


def _swa(q, kv, sinks, *, layer, n_sub):
    n_batch, seq, width = q.shape
    blk = WINDOW
    n_q_heads = width // HEAD_DIM
    assert 2 * HEAD_DIM == LANES and N_KV_HEADS == 2 and kv.shape[-1] == 4 * LANES
    assert seq % (n_sub * blk) == 0
    prev = lambda b, n: (b, jnp.maximum(n_sub * n - 1, 0), 0)
    cur = lambda b, n: (b, n, 0)
    return pl.pallas_call(
        functools.partial(_swa_kernel, layer=layer, n_q_heads=n_q_heads, blk=blk),
        grid=(n_batch, seq // (n_sub * blk)),
        in_specs=[
            pl.BlockSpec(memory_space=pltpu.SMEM),
            pl.BlockSpec((None, n_sub * blk, width), cur),
            pl.BlockSpec((None, blk, kv.shape[-1]), prev),
            pl.BlockSpec((None, n_sub * blk, kv.shape[-1]), cur),
        ],
        out_specs=pl.BlockSpec((None, n_sub * blk, width), cur),
        out_shape=jax.ShapeDtypeStruct((n_batch, seq, width), BF16),
        compiler_params=_params(2),
        name="swa",
    )(sinks, q, kv, kv)


def _merge_ln_kernel(h_ref, yr_ref, ya_ref, win_ref, wbr_ref, wba_ref, wo_ref, g_ref, b_ref, o_ref,
                     merged_s, *, gate_off):
    d_model = h_ref.shape[1]
    h = h_ref[...]
    hb = h.astype(BF16)
    yr = yr_ref[...]
    ya = ya_ref[...]
    for c in range(d_model // MXU_DIM):
        cs = slice(c * MXU_DIM, (c + 1) * MXU_DIM)
        gs_r = slice(gate_off + c * MXU_DIM, gate_off + (c + 1) * MXU_DIM)
        gs_a = slice(gate_off + d_model + c * MXU_DIM, gate_off + d_model + (c + 1) * MXU_DIM)
        gate_r = _sigmoid(_dot(hb, win_ref[:, gs_r]))
        gate_a = _sigmoid(_dot(hb, win_ref[:, gs_a]))
        merged = gate_r * _dot(yr, wbr_ref[:, cs]) + gate_a * _dot(ya, wba_ref[:, cs])
        merged_s[:, cs] = merged.astype(BF16)
    _project_residual_norm(h_ref, merged_s, wo_ref, g_ref, b_ref, o_ref)


def _merge_ln(h2, yr2, ya2, w_in, w_br, w_ba, w_o, ln_g, ln_b, *, layer, w_layer, gate_off, tm):
    n_tok, d_model = h2.shape
    tile = lambda: pl.BlockSpec((tm, d_model), lambda i: (i, 0))
    return pl.pallas_call(
        functools.partial(_merge_ln_kernel, gate_off=gate_off),
        grid=(n_tok // tm,),
        in_specs=[tile(), tile(), tile(),
                  *[_layer_spec(p, w_layer) for p in (w_in, w_br, w_ba, w_o)],
                  _layer_spec(ln_g, layer), _layer_spec(ln_b, layer)],
        out_specs=tile(),
        out_shape=jax.ShapeDtypeStruct((n_tok, d_model), F32),
        scratch_shapes=[pltpu.VMEM((tm, d_model), BF16)],
        compiler_params=_params(1),
        name="merge_ln",
    )(h2, yr2, ya2, w_in, w_br, w_ba, w_o, ln_g, ln_b)


def _cross_ln_kernel(h_ref, mem_ref, wq_ref, wkv_ref, wo_ref, g_ref, b_ref, *refs, n_casts):
    cast_in, o_ref, cast_out = refs[:n_casts], refs[n_casts], refs[n_casts + 1:2 * n_casts + 1]
    qk_s, vo_s, p_s = refs[2 * n_casts + 1:]
    for src, dst in zip(cast_in, cast_out):
        dst[...] = src[...].astype(BF16)
    d_model = h_ref.shape[1]
    mem_len = mem_ref.shape[0]
    head_dim = d_model // CROSS_HEADS
    scale = head_dim ** -0.5

    @pl.when(pl.program_id(1) == 0)
    def _():
        kv = _dot(mem_ref[...].astype(BF16), wkv_ref[...])
        for hd in range(CROSS_HEADS):
            cs = slice(hd * head_dim, (hd + 1) * head_dim)
            ms = slice(hd * mem_len, (hd + 1) * mem_len)
            k = (kv[:, cs] * scale).astype(BF16)
            v = kv[:, d_model + hd * head_dim:d_model + (hd + 1) * head_dim].astype(BF16)
            qk_s[:, ms] = _dot_nt(wq_ref[:, cs], k).astype(BF16)
            vo_s[ms, :] = _dot(v, wo_ref[cs, :]).astype(BF16)

    h = h_ref[...]
    s = _dot(h.astype(BF16), qk_s[...])
    for hd in range(CROSS_HEADS):
        ms = slice(hd * mem_len, (hd + 1) * mem_len)
        sh = s[:, ms]
        e = jnp.exp(sh - jnp.max(sh, axis=-1, keepdims=True))
        p_s[:, ms] = (e * (1.0 / jnp.sum(e, axis=-1, keepdims=True))).astype(BF16)
    out = _dot(p_s[...], vo_s[...])
    o_ref[...] = _layer_norm(DEEPNORM_ALPHA * h + out, g_ref[...], b_ref[...])


def _cross_ln(h, mem, w_q, w_kv, w_o, ln_g, ln_b, *, layer, w_layer, tm, cast_stacks=(),
              cast_layer=None):
    n_batch, seq, d_model = h.shape
    mem_len = mem.shape[1]
    per_batch = seq // tm
    tile = pl.BlockSpec((None, tm, d_model), lambda b, i: (b, i, 0))
    cast_in, cast_out, cast_shapes = _next_layer_cast(
        cast_stacks, cast_layer, n_batch * per_batch, lambda b, i: b * per_batch + i)
    out, *casts = pl.pallas_call(
        functools.partial(_cross_ln_kernel, n_casts=len(cast_stacks)),
        grid=(n_batch, per_batch),
        in_specs=[tile,
                  pl.BlockSpec((None, mem_len, d_model), lambda b, i: (b, 0, 0)),
                  *[_layer_spec(p, w_layer) for p in (w_q, w_kv, w_o)],
                  _layer_spec(ln_g, layer), _layer_spec(ln_b, layer), *cast_in],
        out_specs=[tile, *cast_out],
        out_shape=[jax.ShapeDtypeStruct((n_batch, seq, d_model), F32), *cast_shapes],
        scratch_shapes=[pltpu.VMEM((d_model, CROSS_HEADS * mem_len), BF16),
                        pltpu.VMEM((CROSS_HEADS * mem_len, d_model), BF16),
                        pltpu.VMEM((tm, CROSS_HEADS * mem_len), BF16)],
        compiler_params=_params(2),
        name="cross_ln",
    )(h, mem, w_q, w_kv, w_o, ln_g, ln_b, *cast_stacks)
    return out, casts


def _ffn_ln_kernel(h_ref, wi_ref, wo_ref, g_ref, b_ref, *refs, n_casts):
    cast_in, o_ref, cast_out = refs[:n_casts], refs[n_casts], refs[n_casts + 1:2 * n_casts + 1]
    act_s, = refs[2 * n_casts + 1:]
    for src, dst in zip(cast_in, cast_out):
        dst[...] = src[...].astype(BF16)
    d_ff = wo_ref.shape[0]
    h = h_ref[...]
    hb = h.astype(BF16)
    for c in range(d_ff // MXU_DIM):
        cs = slice(c * MXU_DIM, (c + 1) * MXU_DIM)
        cs_up = slice(d_ff + c * MXU_DIM, d_ff + (c + 1) * MXU_DIM)
        gate = _dot(hb, wi_ref[:, cs])
        up = _dot(hb, wi_ref[:, cs_up])
        act_s[:, cs] = (jax.nn.silu(gate) * up).astype(BF16)
    _project_residual_norm(h_ref, act_s, wo_ref, g_ref, b_ref, o_ref)


def _ffn_ln(h2, w_i, w_o, ln_g, ln_b, *, layer, w_layer, tm, cast_stacks=(), cast_layer=None):
    n_tok, d_model = h2.shape
    d_ff = w_o.shape[1]
    assert d_ff % MXU_DIM == 0
    n_tiles = n_tok // tm
    tile = lambda: pl.BlockSpec((tm, d_model), lambda i: (i, 0))
    cast_in, cast_out, cast_shapes = _next_layer_cast(cast_stacks, cast_layer, n_tiles, lambda i: i)
    out, *casts = pl.pallas_call(
        functools.partial(_ffn_ln_kernel, n_casts=len(cast_stacks)),
        grid=(n_tiles,),
        in_specs=[tile(), _layer_spec(w_i, w_layer), _layer_spec(w_o, w_layer),
                  _layer_spec(ln_g, layer), _layer_spec(ln_b, layer), *cast_in],
        out_specs=[tile(), *cast_out],
        out_shape=[jax.ShapeDtypeStruct((n_tok, d_model), F32), *cast_shapes],
        scratch_shapes=[pltpu.VMEM((tm, d_ff), BF16)],
        compiler_params=_params(1),
        name="ffn_ln",
    )(h2, w_i, w_o, ln_g, ln_b, *cast_stacks)
    return out, casts


def _tile_sizes(seq):
    ts = 128 if seq % 128 == 0 else seq
    tm = 1024 if seq % 1024 == 0 else seq
    return ts, tm


def kernel(x, mem, w_in, conv_w, conv_b, w_rg, b_rg, w_ig, b_ig, lru_lambda, w_br_rnn, w_br_attn, sinks, w_out, ln1_g, ln1_b, cq_w, ckv_w, co_w, ln2_g, ln2_b, ffn_wi, ffn_wo, ln3_g, ln3_b):
    n_batch, seq, d_model = x.shape
    depth = w_in.shape[0]
    n_tok = n_batch * seq
    kv_width = N_KV_HEADS * HEAD_DIM
    n_a = 3 * d_model + 2 * kv_width
    ts, tm = _tile_sizes(seq)
    rope_tabs = _rope_tables(seq)
    rows = lambda p: p.reshape(depth, 1, -1)

    w_gate = jnp.concatenate([w_rg, w_ig], axis=-1).astype(BF16)
    mixer_stacks = (w_in, w_br_rnn, w_br_attn, w_out, cq_w, ckv_w, co_w)
    ffn_stacks = (ffn_wi, ffn_wo)
    w_in_b, w_br, w_ba, w_o, w_cq, w_ckv, w_co, w_fi, w_fo = _cast_first_layer(mixer_stacks + ffn_stacks)

    h = x
    for l in range(depth):
        has_next = l + 1 < depth
        y_rnn, q, kv = _rglru_qkv(h, w_in_b, w_gate, conv_w, rows(conv_b), rows(b_rg), rows(b_ig),
                                    rows(lru_lambda), rope_tabs, layer=l, w_layer=0, ts=ts)
        y_attn = _swa(q, kv, sinks, layer=l, n_sub=8 if seq % (8 * WINDOW) == 0 else 1)
        h2 = _merge_ln(h.reshape(n_tok, d_model), y_rnn.reshape(n_tok, d_model),
                       y_attn.reshape(n_tok, d_model), w_in_b, w_br, w_ba, w_o,
                       rows(ln1_g), rows(ln1_b), layer=l, w_layer=0, gate_off=n_a, tm=tm)
        h, casts = _cross_ln(h2.reshape(n_batch, seq, d_model), mem, w_cq, w_ckv, w_co,
                             rows(ln2_g), rows(ln2_b), layer=l, w_layer=0, tm=tm,
                             cast_stacks=mixer_stacks if has_next else (), cast_layer=l + 1)
        if has_next:
            w_in_b, w_br, w_ba, w_o, w_cq, w_ckv, w_co = casts
        h2, casts = _ffn_ln(h.reshape(n_tok, d_model), w_fi, w_fo, rows(ln3_g), rows(ln3_b),
                            layer=l, w_layer=0, tm=tm,
                            cast_stacks=ffn_stacks if has_next else (), cast_layer=l + 1)
        if has_next:
            w_fi, w_fo = casts
        h = h2.reshape(n_batch, seq, d_model)
    return h
```

```python
import functools
import math

import jax
import jax.numpy as jnp
from jax import lax
from jax.experimental import pallas as pl
from jax.experimental.pallas import tpu as pltpu

F32 = jnp.float32
BF16 = jnp.bfloat16

DEPTH = 4
RNN_BLOCKS = 4
CONV_WIDTH = 4
LRU_C = 8.0
HEAD_DIM = 64
N_KV_HEADS = 2
WINDOW = 128
ROPE_THETA = 500000.0
ROT_DIM = HEAD_DIM // 4
CROSS_HEADS = 4
LN_EPS = 1e-5
DEEPNORM_ALPHA = (2 * DEPTH) ** 0.25
NEG_INF = -1e30

SUBLANES = 8
LANES = 128
MXU_DIM = 256
VMEM_LIMIT_BYTES = 56 * 1024 * 1024


def _params(n_grid_dims):
    return pltpu.CompilerParams(
        dimension_semantics=("arbitrary",) * n_grid_dims,
        vmem_limit_bytes=VMEM_LIMIT_BYTES,
    )


def _layer_spec(stack, layer):
    zeros = (0,) * (stack.ndim - 1)
    return pl.BlockSpec((None,) + stack.shape[1:], lambda *_: (layer,) + zeros,
                        pipeline_mode=pl.Buffered(1))


def _next_layer_cast(stacks, layer, n_steps, step_of):
    in_specs, out_specs, out_shapes = [], [], []
    for w in stacks:
        _, k, n = w.shape
        assert k % (n_steps * 2 * SUBLANES) == 0
        slab = k // n_steps
        in_specs.append(pl.BlockSpec((None, slab, n), lambda *g: (layer, step_of(*g), 0)))
        out_specs.append(pl.BlockSpec((None, slab, n), lambda *g: (0, step_of(*g), 0)))
        out_shapes.append(jax.ShapeDtypeStruct((1, k, n), BF16))
    return in_specs, out_specs, out_shapes


def _cast_kernel(*refs):
    n = len(refs) // 2
    for src, dst in zip(refs[:n], refs[n:]):
        dst[...] = src[...].astype(BF16)


def _cast_first_layer(stacks, n_steps=8):
    in_specs, out_specs, out_shapes = _next_layer_cast(stacks, 0, n_steps, lambda i: i)
    return pl.pallas_call(
        _cast_kernel,
        grid=(n_steps,),
        in_specs=in_specs,
        out_specs=out_specs,
        out_shape=out_shapes,
        compiler_params=_params(1),
        name="cast_first_layer",
    )(*stacks)


def _layer_norm(x, g, b):
    mu = jnp.mean(x, axis=-1, keepdims=True)
    xc = x - mu
    var = jnp.mean(xc * xc, axis=-1, keepdims=True)
    return xc * lax.rsqrt(var + LN_EPS) * g + b


def _project_residual_norm(h_ref, x_s, w_ref, g_ref, b_ref, o_ref, n_split=4):
    rows = h_ref.shape[0] // n_split
    blocks = [slice(j * rows, (j + 1) * rows) for j in range(n_split)]
    outs = [_dot(x_s[rs, :], w_ref[...]) for rs in blocks]
    for rs, out in zip(blocks, outs):
        o_ref[rs, :] = _layer_norm(DEEPNORM_ALPHA * h_ref[rs, :] + out, g_ref[...], b_ref[...])


def _sigmoid(x):
    return 0.5 * jnp.tanh(0.5 * x) + 0.5


def _sqrt_nonneg(x):
    return jnp.where(x > 0.0, x * lax.rsqrt(x), 0.0)


def _dot(a, b):
    return jnp.dot(a, b, preferred_element_type=F32)


def _dot_nt(a, b):
    return lax.dot_general(a, b, (((1,), (1,)), ((), ())), preferred_element_type=F32)


def _rglru_qkv_kernel(x_ref, w_ref, wgate_ref, convw_ref, convb_ref, brg_ref, big_ref, lam_ref,
                      rc_ref, rs1_ref, rs2_ref,
                      y_ref, q_ref, kv_ref,
                      tm_s, stage_s, ab_s, convc_s, hc_s, *, n_batch, ts, d_model, kv_width):
    rows = n_batch * ts
    rnn_block = d_model // RNN_BLOCKS
    tail = (CONV_WIDTH - 1) * n_batch
    pitch = _batch_pitch(ts)

    @pl.when(pl.program_id(0) == 0)
    def _():
        convc_s[...] = jnp.zeros_like(convc_s)
        hc_s[...] = jnp.zeros_like(hc_s)

    xs = jnp.concatenate([x_ref[b] for b in range(n_batch)], axis=0).astype(BF16)

    def to_time_major(val, slot):
        n_blk = val.shape[1] // LANES
        for j in range(n_blk):
            for b in range(n_batch):
                tm_s[slot * n_blk + j, b * pitch:b * pitch + ts, :] = (
                    val[b * ts:(b + 1) * ts, j * LANES:(j + 1) * LANES])
        return jnp.concatenate(
            [jnp.concatenate([tm_s[slot * n_blk + j, pl.ds(t, n_batch, stride=pitch), :]
                              for j in range(n_blk)], axis=1)
             for t in range(ts)], axis=0)

    rc, rs1, rs2 = rc_ref[...], rs1_ref[...], rs2_ref[...]
    half = ROT_DIM // 2

    def rope(tile, tabs):
        c, s1, s2 = tabs
        return (tile * c + pltpu.roll(tile, LANES - half, axis=1) * s1
                + pltpu.roll(tile, half, axis=1) * s2)

    q_off = 2 * d_model
    scale = HEAD_DIM ** -0.5
    k_tabs = (rc, rs1, rs2)
    q_tabs = (rc * scale, rs1 * scale, rs2 * scale)

    def project_q(c):
        return _dot(xs, w_ref[:, q_off + c * MXU_DIM:q_off + (c + 1) * MXU_DIM])

    def emit_q(c, qc):
        for b in range(n_batch):
            qb = qc[b * ts:(b + 1) * ts, :]
            parts = [rope(qb[:, j * LANES:(j + 1) * LANES], q_tabs) for j in range(MXU_DIM // LANES)]
            q_ref[b, :, c * MXU_DIM:(c + 1) * MXU_DIM] = jnp.concatenate(parts, axis=1).astype(BF16)

    def project_kv():
        kv_off = 3 * d_model
        return _dot(xs, w_ref[:, kv_off:kv_off + 2 * kv_width])

    def emit_kv(kv):
        for b in range(n_batch):
            kvb = kv[b * ts:(b + 1) * ts, :]
            kb = rope(kvb[:, :kv_width], k_tabs)
            vb = kvb[:, kv_width:]
            kv_ref[b] = jnp.concatenate(
                [kb, pltpu.roll(kb, HEAD_DIM, axis=1), vb, pltpu.roll(vb, HEAD_DIM, axis=1)],
                axis=1).astype(BF16)

    neg_c_softplus = -LRU_C * jax.nn.softplus(-lam_ref[...])

    for n in range(RNN_BLOCKS):
        cs = slice(n * rnn_block, (n + 1) * rnn_block)
        xr_rows = _dot(xs, w_ref[:, cs])
        qc = project_q(n)
        xr = to_time_major(xr_rows, n)
        hist = convc_s[:, cs]
        ext = jnp.concatenate([hist, xr], axis=0)
        convc_s[:, cs] = xr[rows - tail:, :]
        xc = convb_ref[:, cs]
        for kk in range(CONV_WIDTH):
            xc = xc + ext[kk * n_batch:kk * n_batch + rows, :] * convw_ref[kk:kk + 1, cs]
        gates = _dot(xc.astype(BF16), wgate_ref[n])
        emit_q(n, qc)
        r = _sigmoid(gates[:, :rnn_block] + brg_ref[:, cs])
        i = _sigmoid(gates[:, rnn_block:] + big_ref[:, cs])
        log_a = neg_c_softplus[:, cs] * r
        a = jnp.exp(log_a)
        mult = _sqrt_nonneg(-jnp.tanh(log_a) * (a * a + 1.0))
        ab_s[0, :, cs] = a
        ab_s[1, :, cs] = mult * (i * xc)

    h = hc_s[...]
    for t in range(ts):
        rs = slice(t * n_batch, (t + 1) * n_batch)
        h = ab_s[0, rs, :] * h + ab_s[1, rs, :]
        for j in range(d_model // LANES):
            stage_s[j, rs, :] = h[:, j * LANES:(j + 1) * LANES]
    hc_s[...] = h

    blk_lanes = rnn_block // LANES
    for n in range(RNN_BLOCKS):
        gate = jax.nn.gelu(
            _dot(xs, w_ref[:, d_model + n * rnn_block:d_model + (n + 1) * rnn_block]))
        for b in range(n_batch):
            hb = jnp.concatenate(
                [stage_s[n * blk_lanes + j, pl.ds(b, ts, stride=n_batch), :] for j in range(blk_lanes)],
                axis=1)
            y_ref[b, :, n * rnn_block:(n + 1) * rnn_block] = (
                hb * gate[b * ts:(b + 1) * ts, :]).astype(BF16)
    emit_kv(project_kv())


def _batch_pitch(ts):
    groups = ts // SUBLANES
    return SUBLANES * (groups + 1 if groups % 2 == 0 else groups)


def _rope_tables(seq_len):
    half = ROT_DIM // 2
    pos = jnp.arange(seq_len, dtype=F32)
    inv_freq = ROPE_THETA ** (-jnp.arange(0, ROT_DIM, 2, dtype=F32) / ROT_DIM)
    ang = pos[:, None] * inv_freq[None, :]
    cos, sin = jnp.cos(ang), jnp.sin(ang)
    ones = jnp.ones((seq_len, HEAD_DIM - ROT_DIM), F32)
    zeros = jnp.zeros((seq_len, HEAD_DIM - ROT_DIM), F32)
    zh = jnp.zeros((seq_len, half), F32)
    c_head = jnp.concatenate([cos, cos, ones], axis=1)
    s1_head = jnp.concatenate([-sin, zh, zeros], axis=1)
    s2_head = jnp.concatenate([zh, sin, zeros], axis=1)
    reps = LANES // HEAD_DIM
    return (jnp.tile(c_head, (1, reps)), jnp.tile(s1_head, (1, reps)), jnp.tile(s2_head, (1, reps)))


def _rglru_qkv(h, w_a, w_gate, conv_w, conv_b, b_rg, b_ig, lam, rope_tabs, *, layer, w_layer, ts):
    n_batch, seq, d_model = h.shape
    kv_width = N_KV_HEADS * HEAD_DIM
    assert n_batch == SUBLANES and seq % ts == 0 and ts % SUBLANES == 0 and kv_width == LANES
    rows = n_batch * ts
    kernel = functools.partial(_rglru_qkv_kernel, n_batch=n_batch, ts=ts, d_model=d_model,
                               kv_width=kv_width)
    tile3 = lambda width: pl.BlockSpec((n_batch, ts, width), lambda i: (0, i, 0))
    rope_spec = pl.BlockSpec((ts, LANES), lambda i: (i, 0))
    return pl.pallas_call(
        kernel,
        grid=(seq // ts,),
        in_specs=[
            tile3(d_model),
            _layer_spec(w_a, w_layer),
            *[_layer_spec(p, layer) for p in (w_gate, conv_w, conv_b, b_rg, b_ig, lam)],
            rope_spec, rope_spec, rope_spec,
        ],
        out_specs=[tile3(d_model), tile3(d_model), tile3(4 * kv_width)],
        out_shape=[
            jax.ShapeDtypeStruct((n_batch, seq, d_model), BF16),
            jax.ShapeDtypeStruct((n_batch, seq, d_model), BF16),
            jax.ShapeDtypeStruct((n_batch, seq, 4 * kv_width), BF16),
        ],
        scratch_shapes=[
            pltpu.VMEM((d_model // LANES, n_batch * _batch_pitch(ts), LANES), F32),
            pltpu.VMEM((d_model // LANES, rows, LANES), F32),
            pltpu.VMEM((2, rows, d_model), F32),
            pltpu.VMEM(((CONV_WIDTH - 1) * n_batch, d_model), F32),
            pltpu.VMEM((n_batch, d_model), F32),
        ],
        compiler_params=_params(1),
        name="rglru_qkv",
    )(h, w_a, w_gate, conv_w, conv_b, b_rg, b_ig, lam, *rope_tabs)


def _swa_kernel(sinks_ref, q_ref, kvp_ref, kvc_ref, o_ref, *, layer, n_q_heads, blk):
    n_sub = q_ref.shape[0] // blk
    group = n_q_heads // N_KV_HEADS
    pairs = group // 2
    row = lax.broadcasted_iota(jnp.int32, (blk, blk), 0)
    col = lax.broadcasted_iota(jnp.int32, (blk, blk), 1)
    upper = col > row
    no_prev = jnp.where(pl.program_id(1) == 0, NEG_INF, 0.0)
    first_bias = jnp.where(upper, no_prev, 0.0)
    low_lanes = lax.broadcasted_iota(jnp.int32, (blk, LANES), 1) < HEAD_DIM
    low2 = lax.broadcasted_iota(jnp.int32, (2 * blk, LANES), 1) < HEAD_DIM
    head_ind = jnp.concatenate([jnp.where(low2, 1.0, 0.0), jnp.where(low2, 0.0, 1.0)],
                               axis=0).astype(BF16)

    def keep(x, low):
        return jnp.where(low_lanes if low else ~low_lanes, x, jnp.zeros_like(x))

    for sb in range(n_sub):
        rows = slice(sb * blk, (sb + 1) * blk)

        def prev(j):
            if sb == 0:
                return kvp_ref[:, j * LANES:(j + 1) * LANES]
            return kvc_ref[(sb - 1) * blk:sb * blk, j * LANES:(j + 1) * LANES]

        def cur(j):
            return kvc_ref[rows, j * LANES:(j + 1) * LANES]

        for g in range(N_KV_HEADS):
            low_src, high_src = (0, 1) if g == 0 else (1, 0)
            kk = jnp.concatenate([keep(prev(low_src), True), keep(cur(low_src), True),
                                  keep(prev(high_src), False), keep(cur(high_src), False)], axis=0)
            vv = jnp.concatenate([keep(prev(2 + low_src), True), keep(cur(2 + low_src), True),
                                  keep(prev(2 + high_src), False), keep(cur(2 + high_src), False)], axis=0)
            qg = jnp.concatenate(
                [q_ref[rows, (g * pairs + p) * LANES:(g * pairs + p + 1) * LANES] for p in range(pairs)], axis=0)
            s = _dot_nt(qg, kk)
            probs, sink_logits = [], []
            for p in range(pairs):
                sp = s[p * blk:(p + 1) * blk, :]
                row_probs, rel = [], []
                for half in range(2):
                    sink = sinks_ref[layer, g * group + 2 * p + half]
                    comb = jnp.where(upper, sp[:, 2 * half * blk:(2 * half + 1) * blk],
                                     sp[:, (2 * half + 1) * blk:(2 * half + 2) * blk])
                    if sb == 0:
                        comb = comb + first_bias
                    m = jnp.maximum(jnp.max(comb, axis=-1, keepdims=True), sink)
                    e = jnp.exp(comb - m)
                    rel.append(sink - m)
                    row_probs += [jnp.where(upper, e, 0.0).astype(BF16), jnp.where(upper, 0.0, e).astype(BF16)]
                probs.append(jnp.concatenate(row_probs, axis=1))
                sink_logits.append(jnp.where(low_lanes, rel[0], rel[1]))
            o = _dot(jnp.concatenate(probs, axis=0), jnp.concatenate([vv, head_ind], axis=1))
            for p in range(pairs):
                prs = slice(p * blk, (p + 1) * blk)
                denom = o[prs, LANES:] + jnp.exp(sink_logits[p])
                j = g * pairs + p
                o_ref[rows, j * LANES:(j + 1) * LANES] = (o[prs, :LANES] * (1.0 / denom)).astype(o_ref.dtype)


def _swa(q, kv, sinks, *, layer, n_sub):
    n_batch, seq, width = q.shape
    blk = WINDOW
    n_q_heads = width // HEAD_DIM
    assert 2 * HEAD_DIM == LANES and N_KV_HEADS == 2 and kv.shape[-1] == 4 * LANES
    assert seq % (n_sub * blk) == 0
    prev = lambda b, n: (b, jnp.maximum(n_sub * n - 1, 0), 0)
    cur = lambda b, n: (b, n, 0)
    return pl.pallas_call(
        functools.partial(_swa_kernel, layer=layer, n_q_heads=n_q_heads, blk=blk),
        grid=(n_batch, seq // (n_sub * blk)),
        in_specs=[
            pl.BlockSpec(memory_space=pltpu.SMEM),
            pl.BlockSpec((None, n_sub * blk, width), cur),
            pl.BlockSpec((None, blk, kv.shape[-1]), prev),
            pl.BlockSpec((None, n_sub * blk, kv.shape[-1]), cur),
        ],
        out_specs=pl.BlockSpec((None, n_sub * blk, width), cur),
        out_shape=jax.ShapeDtypeStruct((n_batch, seq, width), BF16),
        compiler_params=_params(2),
        name="swa",
    )(sinks, q, kv, kv)


def _merge_ln_kernel(h_ref, yr_ref, ya_ref, win_ref, wbr_ref, wba_ref, wo_ref, g_ref, b_ref, o_ref,
                     merged_s, *, gate_off):
    d_model = h_ref.shape[1]
    h = h_ref[...]
    hb = h.astype(BF16)
    yr = yr_ref[...]
    ya = ya_ref[...]
    for c in range(d_model // MXU_DIM):
        cs = slice(c * MXU_DIM, (c + 1) * MXU_DIM)
        gs_r = slice(gate_off + c * MXU_DIM, gate_off + (c + 1) * MXU_DIM)
        gs_a = slice(gate_off + d_model + c * MXU_DIM, gate_off + d_model + (c + 1) * MXU_DIM)
        gate_r = _sigmoid(_dot(hb, win_ref[:, gs_r]))
        gate_a = _sigmoid(_dot(hb, win_ref[:, gs_a]))
        merged = gate_r * _dot(yr, wbr_ref[:, cs]) + gate_a * _dot(ya, wba_ref[:, cs])
        merged_s[:, cs] = merged.astype(BF16)
    _project_residual_norm(h_ref, merged_s, wo_ref, g_ref, b_ref, o_ref)


def _merge_ln(h2, yr2, ya2, w_in, w_br, w_ba, w_o, ln_g, ln_b, *, layer, w_layer, gate_off, tm):
    n_tok, d_model = h2.shape
    tile = lambda: pl.BlockSpec((tm, d_model), lambda i: (i, 0))
    return pl.pallas_call(
        functools.partial(_merge_ln_kernel, gate_off=gate_off),
        grid=(n_tok // tm,),
        in_specs=[tile(), tile(), tile(),
                  *[_layer_spec(p, w_layer) for p in (w_in, w_br, w_ba, w_o)],
                  _layer_spec(ln_g, layer), _layer_spec(ln_b, layer)],
        out_specs=tile(),
        out_shape=jax.ShapeDtypeStruct((n_tok, d_model), F32),
        scratch_shapes=[pltpu.VMEM((tm, d_model), BF16)],
        compiler_params=_params(1),
        name="merge_ln",
    )(h2, yr2, ya2, w_in, w_br, w_ba, w_o, ln_g, ln_b)


def _cross_ln_kernel(h_ref, mem_ref, wq_ref, wkv_ref, wo_ref, g_ref, b_ref, *refs, n_casts):
    cast_in, o_ref, cast_out = refs[:n_casts], refs[n_casts], refs[n_casts + 1:2 * n_casts + 1]
    qk_s, vo_s, p_s = refs[2 * n_casts + 1:]
    for src, dst in zip(cast_in, cast_out):
        dst[...] = src[...].astype(BF16)
    d_model = h_ref.shape[1]
    mem_len = mem_ref.shape[0]
    head_dim = d_model // CROSS_HEADS
    scale = head_dim ** -0.5

    @pl.when(pl.program_id(1) == 0)
    def _():
        kv = _dot(mem_ref[...].astype(BF16), wkv_ref[...])
        for hd in range(CROSS_HEADS):
            cs = slice(hd * head_dim, (hd + 1) * head_dim)
            ms = slice(hd * mem_len, (hd + 1) * mem_len)
            k = (kv[:, cs] * scale).astype(BF16)
            v = kv[:, d_model + hd * head_dim:d_model + (hd + 1) * head_dim].astype(BF16)
            qk_s[:, ms] = _dot_nt(wq_ref[:, cs], k).astype(BF16)
            vo_s[ms, :] = _dot(v, wo_ref[cs, :]).astype(BF16)

    n_split = 4
    rows = h_ref.shape[0] // n_split
    blocks = [slice(j * rows, (j + 1) * rows) for j in range(n_split)]
    scores = [_dot(h_ref[rs, :].astype(BF16), qk_s[...]) for rs in blocks]
    outs = []
    for j, rs in enumerate(blocks):
        for hd in range(CROSS_HEADS):
            ms = slice(hd * mem_len, (hd + 1) * mem_len)
            sh = scores[j][:, ms]
            e = jnp.exp(sh - jnp.max(sh, axis=-1, keepdims=True))
            p_s[rs, ms] = (e * (1.0 / jnp.sum(e, axis=-1, keepdims=True))).astype(BF16)
        outs.append(_dot(p_s[rs, :], vo_s[...]))
        if j > 0:
            prev = blocks[j - 1]
            o_ref[prev, :] = _layer_norm(DEEPNORM_ALPHA * h_ref[prev, :] + outs[j - 1],
                                         g_ref[...], b_ref[...])
    last = blocks[-1]
    o_ref[last, :] = _layer_norm(DEEPNORM_ALPHA * h_ref[last, :] + outs[-1], g_ref[...], b_ref[...])


def _cross_ln(h, mem, w_q, w_kv, w_o, ln_g, ln_b, *, layer, w_layer, tm, cast_stacks=(),
              cast_layer=None):
    n_batch, seq, d_model = h.shape
    mem_len = mem.shape[1]
    per_batch = seq // tm
    tile = pl.BlockSpec((None, tm, d_model), lambda b, i: (b, i, 0))
    cast_in, cast_out, cast_shapes = _next_layer_cast(
        cast_stacks, cast_layer, n_batch * per_batch, lambda b, i: b * per_batch + i)
    out, *casts = pl.pallas_call(
        functools.partial(_cross_ln_kernel, n_casts=len(cast_stacks)),
        grid=(n_batch, per_batch),
        in_specs=[tile,
                  pl.BlockSpec((None, mem_len, d_model), lambda b, i: (b, 0, 0)),
                  *[_layer_spec(p, w_layer) for p in (w_q, w_kv, w_o)],
                  _layer_spec(ln_g, layer), _layer_spec(ln_b, layer), *cast_in],
        out_specs=[tile, *cast_out],
        out_shape=[jax.ShapeDtypeStruct((n_batch, seq, d_model), F32), *cast_shapes],
        scratch_shapes=[pltpu.VMEM((d_model, CROSS_HEADS * mem_len), BF16),
                        pltpu.VMEM((CROSS_HEADS * mem_len, d_model), BF16),
                        pltpu.VMEM((tm, CROSS_HEADS * mem_len), BF16)],
        compiler_params=_params(2),
        name="cross_ln",
    )(h, mem, w_q, w_kv, w_o, ln_g, ln_b, *cast_stacks)
    return out, casts


def _ffn_ln_kernel(h_ref, wi_ref, wo_ref, g_ref, b_ref, *refs, n_casts):
    cast_in, o_ref, cast_out = refs[:n_casts], refs[n_casts], refs[n_casts + 1:2 * n_casts + 1]
    act_s, = refs[2 * n_casts + 1:]
    for src, dst in zip(cast_in, cast_out):
        dst[...] = src[...].astype(BF16)
    d_ff = wo_ref.shape[0]
    h = h_ref[...]
    hb = h.astype(BF16)
    for c in range(d_ff // MXU_DIM):
        cs = slice(c * MXU_DIM, (c + 1) * MXU_DIM)
        cs_up = slice(d_ff + c * MXU_DIM, d_ff + (c + 1) * MXU_DIM)
        gate = _dot(hb, wi_ref[:, cs])
        up = _dot(hb, wi_ref[:, cs_up])
        act_s[:, cs] = (jax.nn.silu(gate) * up).astype(BF16)
    _project_residual_norm(h_ref, act_s, wo_ref, g_ref, b_ref, o_ref)


def _ffn_ln(h2, w_i, w_o, ln_g, ln_b, *, layer, w_layer, tm, cast_stacks=(), cast_layer=None):
    n_tok, d_model = h2.shape
    d_ff = w_o.shape[1]
    assert d_ff % MXU_DIM == 0
    n_tiles = n_tok // tm
    tile = lambda: pl.BlockSpec((tm, d_model), lambda i: (i, 0))
    cast_in, cast_out, cast_shapes = _next_layer_cast(cast_stacks, cast_layer, n_tiles, lambda i: i)
    out, *casts = pl.pallas_call(
        functools.partial(_ffn_ln_kernel, n_casts=len(cast_stacks)),
        grid=(n_tiles,),
        in_specs=[tile(), _layer_spec(w_i, w_layer), _layer_spec(w_o, w_layer),
                  _layer_spec(ln_g, layer), _layer_spec(ln_b, layer), *cast_in],
        out_specs=[tile(), *cast_out],
        out_shape=[jax.ShapeDtypeStruct((n_tok, d_model), F32), *cast_shapes],
        scratch_shapes=[pltpu.VMEM((tm, d_ff), BF16)],
        compiler_params=_params(1),
        name="ffn_ln",
    )(h2, w_i, w_o, ln_g, ln_b, *cast_stacks)
    return out, casts


def _tile_sizes(seq):
    ts = 128 if seq % 128 == 0 else seq
    tm = 1024 if seq % 1024 == 0 else seq
    return ts, tm


def kernel(x, mem, w_in, conv_w, conv_b, w_rg, b_rg, w_ig, b_ig, lru_lambda, w_br_rnn, w_br_attn, sinks, w_out, ln1_g, ln1_b, cq_w, ckv_w, co_w, ln2_g, ln2_b, ffn_wi, ffn_wo, ln3_g, ln3_b):
    n_batch, seq, d_model = x.shape
    depth = w_in.shape[0]
    n_tok = n_batch * seq
    kv_width = N_KV_HEADS * HEAD_DIM
    n_a = 3 * d_model + 2 * kv_width
    ts, tm = _tile_sizes(seq)
    rope_tabs = _rope_tables(seq)
    rows = lambda p: p.reshape(depth, 1, -1)

    w_gate = jnp.concatenate([w_rg, w_ig], axis=-1).astype(BF16)
    mixer_stacks = (w_in, w_br_rnn, w_br_attn, w_out, cq_w, ckv_w, co_w)
    ffn_stacks = (ffn_wi, ffn_wo)
    w_in_b, w_br, w_ba, w_o, w_cq, w_ckv, w_co, w_fi, w_fo = _cast_first_layer(mixer_stacks + ffn_stacks)

    h = x
    for l in range(depth):
        has_next = l + 1 < depth
        y_rnn, q, kv = _rglru_qkv(h, w_in_b, w_gate, conv_w, rows(conv_b), rows(b_rg), rows(b_ig),
                                    rows(lru_lambda), rope_tabs, layer=l, w_layer=0, ts=ts)
        y_attn = _swa(q, kv, sinks, layer=l, n_sub=8 if seq % (8 * WINDOW) == 0 else 1)
        h2 = _merge_ln(h.reshape(n_tok, d_model), y_rnn.reshape(n_tok, d_model),
                       y_attn.reshape(n_tok, d_model), w_in_b, w_br, w_ba, w_o,
                       rows(ln1_g), rows(ln1_b), layer=l, w_layer=0, gate_off=n_a, tm=tm)
        h, casts = _cross_ln(h2.reshape(n_batch, seq, d_model), mem, w_cq, w_ckv, w_co,
                             rows(ln2_g), rows(ln2_b), layer=l, w_layer=0, tm=tm,
                             cast_stacks=mixer_stacks if has_next else (), cast_layer=l + 1)
        if has_next:
            w_in_b, w_br, w_ba, w_o, w_cq, w_ckv, w_co = casts
        h2, casts = _ffn_ln(h.reshape(n_tok, d_model), w_fi, w_fo, rows(ln3_g), rows(ln3_b),
                            layer=l, w_layer=0, tm=tm,
                            cast_stacks=ffn_stacks if has_next else (), cast_layer=l + 1)
        if has_next:
            w_fi, w_fo = casts
        h = h2.reshape(n_batch, seq, d_model)
    return h
```
